```python
import math
import jax, jax.numpy as jnp
from jax import lax
import numpy as np

D_MODEL = 2048
BATCH = 4
SEQ = 8192
DEPTH = 4

HEAD_DIM = 64
N_BRANCHES = 4
BRANCH_WIDTH = D_MODEL // N_BRANCHES
MIX_HEADS = BRANCH_WIDTH // HEAD_DIM
SB_HEADS = 4
SB_HEAD_DIM = BRANCH_WIDTH // SB_HEADS
A_WINDOWS = (128, 512, 2048)
A_DILATIONS = (1, 4, 16)
MOBA_BLOCK = 256
MOBA_TOPK = 3
MOBA_QBLOCK = 128
SB_BLOCK = 128
SWA_WINDOW = 128
D_KV_HEADS = 2
BAND_BLOCK = 128
REL_BUCKETS = 32
REL_MAX_EXACT = 16
REL_MAX_DIST = 2048
N_BIAS_HEADS = 3 * MIX_HEADS
N_EXPERTS = 64
TOP_K = 8
N_EXPERT_GROUPS = 8
TOPK_GROUPS = 4
EXPERT_HIDDEN = D_MODEL * 3 // 16
ROUTED_SCALE = 2.5
EXPERT_BLOCK = 512
LN_EPS = 1e-5
ALPHA = (2 * DEPTH) ** 0.25
BETA_INIT = (8 * DEPTH) ** -0.25
NEG_INF = -1e30
KV_W = D_KV_HEADS * HEAD_DIM
IN_SPLITS = (BRANCH_WIDTH,) * 10 + (KV_W, KV_W, N_BRANCHES * D_MODEL)
IN_WIDTH = 10 * BRANCH_WIDTH + 2 * KV_W + N_BRANCHES * D_MODEL

kernel_name = 'hybrid_gated_mixers_moe_deepnorm'


def layer_norm(x, g, b):
    xf = x.astype(jnp.float32)
    mu = xf.mean(-1, keepdims=True)
    var = jnp.square(xf - mu).mean(-1, keepdims=True)
    return ((xf - mu) * lax.rsqrt(var + LN_EPS) * g.astype(jnp.float32) + b.astype(jnp.float32)).astype(x.dtype)


def t5_bucket(dist):
    n = jnp.maximum(dist, 0)
    nf = jnp.maximum(n, 1).astype(jnp.float32)
    large = REL_MAX_EXACT + (jnp.log(nf / REL_MAX_EXACT) / math.log(REL_MAX_DIST / REL_MAX_EXACT) * (REL_BUCKETS - REL_MAX_EXACT)).astype(jnp.int32)
    return jnp.where(n < REL_MAX_EXACT, n, jnp.minimum(large, REL_BUCKETS - 1))


def to_residues(t, d):
    b, s = t.shape[:2]
    rest = t.shape[2:]
    return t.reshape(b, s // d, d, *rest).swapaxes(1, 2).reshape(b * d, s // d, *rest)


def from_residues(t, d, batch):
    l = t.shape[1]
    rest = t.shape[2:]
    return t.reshape(batch, d, l, *rest).swapaxes(1, 2).reshape(batch, l * d, *rest)


def banded_attention(q, k, v, max_steps, dist_scale, bias_tab):
    b, l, h, hd = q.shape
    hk = k.shape[2]
    r = h // hk
    blk = BAND_BLOCK
    nb = -(-l // blk)
    lp = nb * blk
    if lp != l:
        padw = ((0, 0), (0, lp - l), (0, 0), (0, 0))
        q, k, v = jnp.pad(q, padw), jnp.pad(k, padw), jnp.pad(v, padw)
    qb = q.reshape(b, nb, blk, hk, r, hd)

    def with_prev(t):
        t = t.reshape(b, nb, blk, hk, hd)
        prev = jnp.pad(t[:, :-1], ((0, 0), (1, 0), (0, 0), (0, 0), (0, 0)))
        return jnp.concatenate([prev, t], axis=2)

    kk, vv = with_prev(k), with_prev(v)
    s = jnp.einsum('bnqgrd,bnkgd->bngrqk', qb, kk).astype(jnp.float32) * (hd ** -0.5)
    qi = jnp.arange(blk)[:, None]
    kj = jnp.arange(2 * blk)[None, :]
    dist = blk + qi - kj
    band = (dist >= 0) & (dist <= max_steps)
    has_prev = (jnp.arange(nb)[:, None, None] > 0) | (kj[None] >= blk)
    mask = band[None] & has_prev
    bias = bias_tab[t5_bucket(dist * dist_scale)].astype(jnp.float32)
    bias = bias.transpose(2, 0, 1).reshape(hk, r, blk, 2 * blk)
    s = jnp.where(mask[None, :, None, None], s + bias[None, None], NEG_INF)
    m = s.max(-1, keepdims=True)
    p = jnp.exp(s - m)
    den = p.sum(-1, keepdims=True)
    o = jnp.einsum('bngrqk,bnkgd->bnqgrd', p / den, vv.astype(jnp.float32)).reshape(b, lp, h, hd)[:, :l]
    lse = (m + jnp.log(den))[..., 0].transpose(0, 1, 4, 2, 3).reshape(b, lp, h)[:, :l]
    return o, lse


def dilated_attention(q, k, v, bias_tab):
    b = q.shape[0]
    outs, lses = [], []
    for win, dil in zip(A_WINDOWS, A_DILATIONS):
        o, lse = banded_attention(to_residues(q, dil), to_residues(k, dil), to_residues(v, dil),
                                  win // dil, dil, bias_tab)
        outs.append(from_residues(o, dil, b))
        lses.append(from_residues(lse, dil, b))
    w = jax.nn.softmax(jnp.stack(lses), axis=0)
    return jnp.einsum('gbsh,gbshd->bshd', w, jnp.stack(outs)).astype(q.dtype)


def moba_selected_blocks(q, kb, vb, idx, tab):
    sp, kk = idx.shape
    nbk, _, hd = kb.shape
    npair = sp * kk
    nc = -(-npair // MOBA_QBLOCK) + nbk
    e_flat = idx.reshape(-1)
    t_flat = jnp.arange(npair, dtype=jnp.int32) // kk
    order = jnp.argsort(e_flat)
    e_s, t_s = e_flat[order], t_flat[order]
    counts = jnp.bincount(e_flat, length=nbk)
    starts = jnp.cumsum(counts) - counts
    pcounts = (counts + MOBA_QBLOCK - 1) // MOBA_QBLOCK * MOBA_QBLOCK
    pends = jnp.cumsum(pcounts)
    pstarts = pends - pcounts
    dest_sorted = pstarts[e_s] + (jnp.arange(npair, dtype=jnp.int32) - starts[e_s])
    slot_tok = jnp.zeros((nc * MOBA_QBLOCK,), jnp.int32).at[dest_sorted].set(t_s)
    dest = jnp.zeros((npair,), jnp.int32).at[order].set(dest_sorted.astype(jnp.int32))
    chunk_blk = jnp.minimum(jnp.searchsorted(pends, jnp.arange(nc) * MOBA_QBLOCK, side='right'), nbk - 1)
    qs = q[slot_tok].reshape(nc, MOBA_QBLOCK, hd)
    ks, vs = kb[chunk_blk], vb[chunk_blk]
    sc = jnp.einsum('cqd,ckd->cqk', qs, ks).astype(jnp.float32) * (hd ** -0.5)
    kpos = chunk_blk[:, None] * MOBA_BLOCK + jnp.arange(MOBA_BLOCK)
    sc = sc + tab[t5_bucket(slot_tok.reshape(nc, MOBA_QBLOCK)[:, :, None] - kpos[:, None, :])]
    m = sc.max(-1, keepdims=True)
    p = jnp.exp(sc - m)
    den = p.sum(-1, keepdims=True)
    o = jnp.einsum('cqk,ckd->cqd', p / den, vs.astype(jnp.float32)).reshape(nc * MOBA_QBLOCK, hd)
    lse = (m + jnp.log(den)).reshape(nc * MOBA_QBLOCK)
    return o[dest].reshape(sp, kk, hd), lse[dest].reshape(sp, kk)


def moba_attention(q, k, v, bias_tab):
    b, s, h, hd = q.shape
    nbk = -(-s // MOBA_BLOCK)
    sp = nbk * MOBA_BLOCK
    g = b * h
    padw = ((0, 0), (0, sp - s), (0, 0), (0, 0))
    prep = lambda t: jnp.pad(t, padw).transpose(0, 2, 1, 3).reshape(g, sp, hd)
    qg = prep(q)
    kg = prep(k).reshape(g, nbk, MOBA_BLOCK, hd)
    vg = prep(v).reshape(g, nbk, MOBA_BLOCK, hd)
    kmean = kg.mean(axis=2)
    gate = jnp.einsum('gtd,gnd->gtn', qg, kmean).astype(jnp.float32)
    qblk = jnp.arange(sp) // MOBA_BLOCK
    past = jnp.arange(nbk)[None, :] < qblk[:, None]
    gate = jnp.where(past[None], gate, NEG_INF)
    topk = min(MOBA_TOPK, nbk)
    _, idx = lax.top_k(gate, topk)
    valid = idx < qblk[None, :, None]
    tab_g = jnp.tile(bias_tab.T.astype(jnp.float32), (b, 1))
    o_sel, lse_sel = jax.vmap(moba_selected_blocks)(qg, kg, vg, idx, tab_g)
    lse_sel = jnp.where(valid, lse_sel, NEG_INF)
    qo = qg.reshape(g, nbk, MOBA_BLOCK, hd)
    s_own = jnp.einsum('gnqd,gnkd->gnqk', qo, kg).astype(jnp.float32) * (hd ** -0.5)
    dist = jnp.arange(MOBA_BLOCK)[:, None] - jnp.arange(MOBA_BLOCK)[None, :]
    bias_own = tab_g[:, t5_bucket(dist)]
    s_own = jnp.where(dist >= 0, s_own + bias_own[:, None], NEG_INF)
    m = s_own.max(-1, keepdims=True)
    p = jnp.exp(s_own - m)
    den = p.sum(-1, keepdims=True)
    o_own = jnp.einsum('gnqk,gnkd->gnqd', p / den, vg.astype(jnp.float32)).reshape(g, sp, hd)
    lse_own = (m + jnp.log(den)).reshape(g, sp, 1)
    w = jax.nn.softmax(jnp.concatenate([lse_sel, lse_own], axis=-1), axis=-1)
    out = jnp.einsum('gtk,gtkd->gtd', w[..., :topk], o_sel) + w[..., topk:] * o_own
    return out.reshape(b, h, sp, hd).transpose(0, 2, 1, 3)[:, :s].astype(q.dtype)


def stick_breaking_attention(q, k, v):
    b, s, h, hd = q.shape
    qh, kh, vh = q.transpose(0, 2, 1, 3), k.transpose(0, 2, 1, 3), v.transpose(0, 2, 1, 3)
    tri = (jnp.arange(SB_BLOCK)[:, None] >= jnp.arange(SB_BLOCK)[None, :]).astype(jnp.float32)
    outs = []
    for i in range(s // SB_BLOCK):
        q0 = i * SB_BLOCK
        ln = q0 + SB_BLOCK
        nkb = i + 1
        z = jnp.einsum('bhqd,bhkd->bhqk', qh[:, :, q0:ln], kh[:, :, :ln]).astype(jnp.float32) * (hd ** -0.5)
        past = jnp.arange(ln)[None, :] < (q0 + jnp.arange(SB_BLOCK))[:, None]
        lk = jnp.where(past, jax.nn.log_sigmoid(-z), 0.0).reshape(b, h, SB_BLOCK, nkb, SB_BLOCK)
        within = jnp.einsum('bhqnj,jk->bhqnk', lk, tri)
        blk_tri = (jnp.arange(nkb)[:, None] > jnp.arange(nkb)[None, :]).astype(jnp.float32)
        carry = jnp.einsum('bhqn,nm->bhqm', lk.sum(-1), blk_tri)
        log_a = (within + carry[..., None]).reshape(b, h, SB_BLOCK, ln) + z
        a = jnp.exp(jnp.where(past, log_a, NEG_INF))
        outs.append(jnp.einsum('bhqk,bhkd->bhqd', a, vh[:, :, :ln].astype(jnp.float32)))
    out = jnp.concatenate(outs, axis=2)
    return out.transpose(0, 2, 1, 3).astype(q.dtype)


def sink_window_attention(q, k, v, sinks, bias_tab):
    o, lse = banded_attention(q, k, v, SWA_WINDOW - 1, 1, bias_tab)
    keep = jax.nn.sigmoid(lse - sinks.astype(jnp.float32))
    return (o * keep[..., None]).astype(q.dtype)


def token_mixers(x, rel_bias, w_in, w_o, w_out, sinks):
    b, s, d = x.shape
    offsets, acc = [], 0
    for w in IN_SPLITS[:-1]:
        acc += w
        offsets.append(acc)
    proj = jnp.einsum('bsd,de->bse', x, w_in)
    qa, ka, va, qb, kb, vb, qc, kc, vc, qd, kd, vd, gates = jnp.split(proj, offsets, axis=-1)
    heads = lambda t: t.reshape(b, s, -1, HEAD_DIM)
    sb_heads = lambda t: t.reshape(b, s, SB_HEADS, SB_HEAD_DIM)
    o_a = dilated_attention(heads(qa), heads(ka), heads(va), rel_bias[:, :MIX_HEADS])
    o_b = moba_attention(heads(qb), heads(kb), heads(vb), rel_bias[:, MIX_HEADS:2 * MIX_HEADS])
    o_c = stick_breaking_attention(sb_heads(qc), sb_heads(kc), sb_heads(vc))
    o_d = sink_window_attention(heads(qd), heads(kd), heads(vd), sinks, rel_bias[:, 2 * MIX_HEADS:])
    gates = jax.nn.sigmoid(gates.reshape(b, s, N_BRANCHES, d))
    w_branch = w_o.reshape(N_BRANCHES, BRANCH_WIDTH, d)
    branches = (o_a, o_b, o_c, o_d)
    merged = gates[:, :, 0] * jnp.einsum('bsc,cd->bsd', branches[0].reshape(b, s, BRANCH_WIDTH), w_branch[0])
    for i in range(1, N_BRANCHES):
        merged = merged + gates[:, :, i] * jnp.einsum('bsc,cd->bsd', branches[i].reshape(b, s, BRANCH_WIDTH), w_branch[i])
    return jnp.einsum('bsd,de->bse', merged, w_out)


def swiglu(x, wg, wu, wd):
    return jnp.dot(jax.nn.silu(jnp.dot(x, wg)) * jnp.dot(x, wu), wd)


def route(xf, w_router, router_bias):
    n = xf.shape[0]
    scores = jax.nn.sigmoid(jnp.einsum('nd,de->ne', xf, w_router).astype(jnp.float32))
    biased = scores + router_bias.astype(jnp.float32)
    group_score = lax.top_k(biased.reshape(n, N_EXPERT_GROUPS, -1), 2)[0].sum(-1)
    _, top_groups = lax.top_k(group_score, TOPK_GROUPS)
    group_mask = (top_groups[..., None] == jnp.arange(N_EXPERT_GROUPS)).any(axis=1)
    masked = jnp.where(jnp.repeat(group_mask, N_EXPERTS // N_EXPERT_GROUPS, axis=1), biased, NEG_INF)
    _, idx = lax.top_k(masked, TOP_K)
    gw = jnp.take_along_axis(scores, idx, axis=1)
    return idx, gw / gw.sum(-1, keepdims=True) * ROUTED_SCALE


def moe_ffn(x, w_router, router_bias, w_gate, w_up, w_down, w_sh_gate, w_sh_up, w_sh_down):
    b, s, d = x.shape
    n = b * s
    xf = x.reshape(n, d)
    idx, gw = route(xf, w_router, router_bias)
    nk = n * TOP_K
    e_flat = idx.reshape(-1)
    t_flat = jnp.arange(nk, dtype=jnp.int32) // TOP_K
    order = jnp.argsort(e_flat)
    e_s, t_s = e_flat[order], t_flat[order]
    counts = jnp.bincount(e_flat, length=N_EXPERTS)
    starts = jnp.cumsum(counts) - counts
    pcounts = (counts + EXPERT_BLOCK - 1) // EXPERT_BLOCK * EXPERT_BLOCK
    pends = jnp.cumsum(pcounts)
    pstarts = pends - pcounts
    dest_sorted = pstarts[e_s] + (jnp.arange(nk, dtype=jnp.int32) - starts[e_s])
    nb = -(-nk // EXPERT_BLOCK) + N_EXPERTS
    tok = jnp.zeros((nb * EXPERT_BLOCK,), jnp.int32).at[dest_sorted].set(t_s)
    dest = jnp.zeros((nk,), jnp.int32).at[order].set(dest_sorted.astype(jnp.int32))
    blk_expert = jnp.minimum(jnp.searchsorted(pends, jnp.arange(nb) * EXPERT_BLOCK, side='right'), N_EXPERTS - 1)

    def block(inp):
        tb, e = inp
        return swiglu(xf[tb], w_gate[e], w_up[e], w_down[e])

    ys = lax.map(block, (tok.reshape(nb, EXPERT_BLOCK), blk_expert)).reshape(nb * EXPERT_BLOCK, d)
    routed = jnp.einsum('nk,nkd->nd', gw.astype(x.dtype), ys[dest].reshape(n, TOP_K, d))
    out = swiglu(xf, w_sh_gate, w_sh_up, w_sh_down) + routed
    return out.reshape(b, s, d)


def setup_inputs(seed: int = 0) -> dict:
    key = jax.random.key(seed)
    ks = jax.random.split(key, 18)
    nrm = lambda k, shape: jax.random.normal(k, shape, jnp.float32)
    d, e, hdn = D_MODEL, N_EXPERTS, EXPERT_HIDDEN
    return {
        'x': nrm(ks[0], (BATCH, SEQ, d)),
        'rel_bias': 0.2 * nrm(ks[1], (REL_BUCKETS, N_BIAS_HEADS)),
        'w_in': nrm(ks[2], (DEPTH, d, IN_WIDTH)) * d ** -0.5,
        'w_o': nrm(ks[3], (DEPTH, N_BRANCHES * BRANCH_WIDTH, d)) * BRANCH_WIDTH ** -0.5,
        'w_out': nrm(ks[4], (DEPTH, d, d)) * (d ** -0.5 * BETA_INIT),
        'attn_sinks': 0.5 * nrm(ks[5], (DEPTH, MIX_HEADS)),
        'ln1_g': 1.0 + 0.02 * nrm(ks[6], (DEPTH, d)),
        'ln1_b': 0.02 * nrm(ks[7], (DEPTH, d)),
        'w_router': nrm(ks[8], (DEPTH, d, e)) * d ** -0.5,
        'router_bias': 0.01 * nrm(ks[9], (DEPTH, e)),
        'w_gate': nrm(ks[10], (DEPTH, e, d, hdn)) * d ** -0.5,
        'w_up': nrm(ks[11], (DEPTH, e, d, hdn)) * d ** -0.5,
        'w_down': nrm(ks[12], (DEPTH, e, hdn, d)) * (hdn ** -0.5 * BETA_INIT),
        'w_sh_gate': nrm(ks[13], (DEPTH, d, hdn)) * d ** -0.5,
        'w_sh_up': nrm(ks[14], (DEPTH, d, hdn)) * d ** -0.5,
        'w_sh_down': nrm(ks[15], (DEPTH, hdn, d)) * (hdn ** -0.5 * BETA_INIT),
        'ln2_g': 1.0 + 0.02 * nrm(ks[16], (DEPTH, d)),
        'ln2_b': 0.02 * nrm(ks[17], (DEPTH, d)),
    }


def reference(x, rel_bias, w_in, w_o, w_out, attn_sinks, ln1_g, ln1_b, w_router, router_bias,
              w_gate, w_up, w_down, w_sh_gate, w_sh_up, w_sh_down, ln2_g, ln2_b):
    for l in range(DEPTH):
        mix = token_mixers(x, rel_bias, w_in[l], w_o[l], w_out[l], attn_sinks[l])
        x = layer_norm(ALPHA * x + mix, ln1_g[l], ln1_b[l])
        ffn = moe_ffn(x, w_router[l], router_bias[l], w_gate[l], w_up[l], w_down[l],
                      w_sh_gate[l], w_sh_up[l], w_sh_down[l])
        x = layer_norm(ALPHA * x + ffn, ln2_g[l], ln2_b[l])
    return x
```

```python
import functools
import math

import numpy as np
import jax
import jax.numpy as jnp
from jax import lax
from jax.experimental import pallas as pl
from jax.experimental.pallas import tpu as pltpu

D_MODEL = 2048
DEPTH = 4
HEAD_DIM = 64
N_BRANCHES = 4
BRANCH_WIDTH = D_MODEL // N_BRANCHES
MIX_HEADS = BRANCH_WIDTH // HEAD_DIM
SB_HEADS = 4
SB_HEAD_DIM = BRANCH_WIDTH // SB_HEADS
A_WINDOWS = (128, 512, 2048)
A_DILATIONS = (1, 4, 16)
MOBA_BLOCK = 256
MOBA_TOPK = 3
SB_BLOCK = 128
SWA_WINDOW = 128
D_KV_HEADS = 2
BAND_BLOCK = 128
REL_BUCKETS = 32
REL_MAX_EXACT = 16
REL_MAX_DIST = 2048
N_EXPERTS = 64
TOP_K = 8
N_EXPERT_GROUPS = 8
TOPK_GROUPS = 4
EXPERT_HIDDEN = D_MODEL * 3 // 16
ROUTED_SCALE = 2.5
EXPERT_BLOCK = 512
LN_EPS = 1e-5
ALPHA = (2 * DEPTH) ** 0.25
NEG_INF = -1e30
KV_W = D_KV_HEADS * HEAD_DIM
QKV_WIDTH = 10 * BRANCH_WIDTH + 2 * KV_W
MOBA_BIAS_TILES = 8

VMEM_LIMIT_BYTES = 56 * 1024 * 1024

F32 = jnp.float32
BF16 = jnp.bfloat16


def _params(semantics, vmem=VMEM_LIMIT_BYTES):
    return pltpu.CompilerParams(dimension_semantics=semantics, vmem_limit_bytes=vmem)


def _dot(a, b):
    return jnp.dot(a, b, preferred_element_type=F32)


def _dot_nt(a, b):
    return lax.dot_general(a, b, (((1,), (1,)), ((), ())), preferred_element_type=F32)


def _mm_kernel(x_ref, w_ref, o_ref):
    o_ref[...] = _dot(x_ref[...], w_ref[...]).astype(o_ref.dtype)


def _matmul(x, w, out_dtype, tm, tn):
    m, k = x.shape
    n = w.shape[1]
    return pl.pallas_call(
        _mm_kernel,
        grid=(m // tm, n // tn),
        in_specs=[pl.BlockSpec((tm, k), lambda i, j: (i, 0)),
                  pl.BlockSpec((k, tn), lambda i, j: (0, j))],
        out_specs=pl.BlockSpec((tm, tn), lambda i, j: (i, j)),
        out_shape=jax.ShapeDtypeStruct((m, n), out_dtype),
        compiler_params=_params(("parallel", "arbitrary")),
        name="qkv_proj",
    )(x, w)


def _t5_bucket(dist):
    n = jnp.maximum(dist, 0)
    nf = jnp.maximum(n, 1).astype(F32)
    large = REL_MAX_EXACT + (jnp.log(nf / REL_MAX_EXACT) / math.log(REL_MAX_DIST / REL_MAX_EXACT)
                             * (REL_BUCKETS - REL_MAX_EXACT)).astype(jnp.int32)
    return jnp.where(n < REL_MAX_EXACT, n, jnp.minimum(large, REL_BUCKETS - 1))


def _band_bias(tab, max_steps, dist_scale):
    qi = np.arange(BAND_BLOCK)[:, None]
    kj = np.arange(2 * BAND_BLOCK)[None, :]
    dist = BAND_BLOCK + qi - kj
    band = (dist >= 0) & (dist <= max_steps)
    bias = tab[_t5_bucket(jnp.asarray(dist * dist_scale, jnp.int32))].astype(F32)
    bias = jnp.where(jnp.asarray(band)[:, :, None], bias, NEG_INF)
    return bias.transpose(2, 0, 1)


def _moba_bias(tab):
    r = np.arange(MOBA_BLOCK)[:, None]
    c = np.arange(MOBA_BLOCK)[None, :]
    dist = np.stack([dl * MOBA_BLOCK + r - c for dl in range(MOBA_BIAS_TILES)])
    bias = tab[_t5_bucket(jnp.asarray(dist, jnp.int32))].astype(F32)
    bias = jnp.where(jnp.asarray(dist >= 0)[..., None], bias, NEG_INF)
    return bias.transpose(3, 0, 1, 2)


def _band_kernel(*refs, kv_rep, with_sink):
    if with_sink:
        q_ref, kp_ref, ko_ref, vp_ref, vo_ref, bias_ref, sink_ref, o_ref = refs
    else:
        q_ref, kp_ref, ko_ref, vp_ref, vo_ref, bias_ref, o_ref, lse_ref = refs
    has_prev = pl.program_id(2) > 0
    q = q_ref[0]
    kp, ko, vp, vo = kp_ref[0], ko_ref[0], vp_ref[0], vo_ref[0]
    for h in range(MIX_HEADS):
        g = h // kv_rep
        hs = slice(h * HEAD_DIM, (h + 1) * HEAD_DIM)
        gs = slice(g * HEAD_DIM, (g + 1) * HEAD_DIM)
        qh = q[:, hs]
        bias = bias_ref[h]
        s_p = _dot_nt(qh, kp[:, gs]) * (HEAD_DIM ** -0.5) + bias[:, :BAND_BLOCK]
        s_o = _dot_nt(qh, ko[:, gs]) * (HEAD_DIM ** -0.5) + bias[:, BAND_BLOCK:]
        s_p = jnp.where(has_prev, s_p, NEG_INF)
        m = jnp.maximum(s_p.max(-1, keepdims=True), s_o.max(-1, keepdims=True))
        p_p = jnp.exp(s_p - m)
        p_o = jnp.exp(s_o - m)
        den = p_p.sum(-1, keepdims=True) + p_o.sum(-1, keepdims=True)
        o = (_dot(p_p.astype(BF16), vp[:, gs]) + _dot(p_o.astype(BF16), vo[:, gs])) / den
        lse = m + jnp.log(den)
        if with_sink:
            keep = jax.nn.sigmoid(lse - sink_ref[:, hs])
            o_ref[0, :, hs] = (o * keep).astype(o_ref.dtype)
        else:
            o_ref[0, :, hs] = o.astype(o_ref.dtype)
            lse_ref[0, :, hs] = jnp.broadcast_to(lse, (BAND_BLOCK, HEAD_DIM))


def _band_attention(q, k, v, bias, dil, sink_row=None):
    b, s, qw = q.shape
    kvw = k.shape[2]
    l = s // dil
    nb = l // BAND_BLOCK
    kv_rep = qw // kvw
    qv = q.reshape(b, l, dil * qw)
    kv_ = k.reshape(b, l, dil * kvw)
    vv = v.reshape(b, l, dil * kvw)
    own = lambda bi, r, n: (bi, n, r)
    prev = lambda bi, r, n: (bi, jnp.maximum(n - 1, 0), r)
    in_specs = [pl.BlockSpec((1, BAND_BLOCK, qw), own),
                pl.BlockSpec((1, BAND_BLOCK, kvw), prev),
                pl.BlockSpec((1, BAND_BLOCK, kvw), own),
                pl.BlockSpec((1, BAND_BLOCK, kvw), prev),
                pl.BlockSpec((1, BAND_BLOCK, kvw), own),
                pl.BlockSpec((MIX_HEADS, BAND_BLOCK, 2 * BAND_BLOCK), lambda bi, r, n: (0, 0, 0))]
    args = [qv, kv_, kv_, vv, vv, bias]
    with_sink = sink_row is not None
    if with_sink:
        in_specs.append(pl.BlockSpec((1, qw), lambda bi, r, n: (0, 0)))
        args.append(sink_row)
        out_shape = jax.ShapeDtypeStruct((b, l, dil * qw), BF16)
        out_specs = pl.BlockSpec((1, BAND_BLOCK, qw), own)
    else:
        out_shape = (jax.ShapeDtypeStruct((b, l, dil * qw), F32),) * 2
        out_specs = (pl.BlockSpec((1, BAND_BLOCK, qw), own),) * 2
    out = pl.pallas_call(
        functools.partial(_band_kernel, kv_rep=kv_rep, with_sink=with_sink),
        grid=(b, dil, nb),
        in_specs=in_specs, out_specs=out_specs, out_shape=out_shape,
        compiler_params=_params(("parallel", "parallel", "arbitrary")),
        name="band_attn_sink" if with_sink else f"band_attn_d{dil}",
    )(*args)
    if with_sink:
        return out.reshape(b, s, qw)
    return out[0].reshape(b, s, qw), out[1].reshape(b, s, qw)


def _mix_combine_kernel(o1, l1, o2, l2, o3, l3, out_ref):
    a, bq, c = l1[...], l2[...], l3[...]
    m = jnp.maximum(jnp.maximum(a, bq), c)
    ea, eb, ec = jnp.exp(a - m), jnp.exp(bq - m), jnp.exp(c - m)
    den = ea + eb + ec
    out_ref[...] = (ea / den * o1[...] + eb / den * o2[...] + ec / den * o3[...]).astype(out_ref.dtype)


def _mix_combine(parts, tm=512):
    n, w = parts[0].shape
    spec = pl.BlockSpec((tm, w), lambda i: (i, 0))
    return pl.pallas_call(
        _mix_combine_kernel, grid=(n // tm,),
        in_specs=[spec] * 6, out_specs=spec,
        out_shape=jax.ShapeDtypeStruct((n, w), BF16),
        compiler_params=_params(("parallel",)),
        name="dilated_combine",
    )(*parts)


def _moba_kernel(q_ref, k_ref, v_ref, bias_ref, o_ref, kmean_ref, *, nbk):
    i = pl.program_id(2)
    pair_w = 2 * HEAD_DIM

    @pl.when(i == 0)
    def _():
        kk = k_ref[0].astype(F32).reshape(nbk, MOBA_BLOCK, pair_w)
        kmean_ref[...] = kk.sum(axis=1) * (1.0 / MOBA_BLOCK)

    q = q_ref[0]
    lane = lax.broadcasted_iota(jnp.int32, (1, pair_w), 1)
    col = lax.broadcasted_iota(jnp.int32, (MOBA_BLOCK, nbk), 1)
    km = kmean_ref[...].astype(BF16)
    outs = []
    for hh in range(2):
        in_head = (lane >= hh * HEAD_DIM) & (lane < (hh + 1) * HEAD_DIM)
        qh = jnp.where(in_head, q, jnp.zeros_like(q))
        gate = jnp.where(col < i, _dot_nt(qh, km), NEG_INF)
        sel = jnp.zeros((MOBA_BLOCK, nbk), F32)
        for _ in range(MOBA_TOPK):
            mx = gate.max(-1, keepdims=True)
            first = jnp.where(gate == mx, col, nbk).min(-1, keepdims=True)
            pick = col == first
            sel = jnp.where(pick & (col < i), 1.0, sel)
            gate = jnp.where(pick, -jnp.inf, gate)

        def scores(j):
            kj = k_ref[0, pl.ds(pl.multiple_of(j * MOBA_BLOCK, MOBA_BLOCK), MOBA_BLOCK), :]
            dl = jnp.minimum(i - j, MOBA_BIAS_TILES - 1)
            return _dot_nt(qh, kj) * (HEAD_DIM ** -0.5) + bias_ref[hh, dl]

        def values(j):
            return v_ref[0, pl.ds(pl.multiple_of(j * MOBA_BLOCK, MOBA_BLOCK), MOBA_BLOCK), :]

        s = scores(i)
        m0 = s.max(-1, keepdims=True)
        p = jnp.exp(s - m0)
        l0 = p.sum(-1, keepdims=True)
        acc0 = _dot(p.astype(BF16), values(i))

        def body(j, carry):
            m, l, acc = carry
            chosen = jnp.where(col == j, sel, 0.0).sum(-1, keepdims=True) > 0.0
            s = jnp.where(chosen, scores(j), NEG_INF)
            m_new = jnp.maximum(m, s.max(-1, keepdims=True))
            scale = jnp.exp(m - m_new)
            p = jnp.exp(s - m_new)
            l = scale * l + p.sum(-1, keepdims=True)
            acc = scale * acc + _dot(p.astype(BF16), values(j))
            return m_new, l, acc

        _, l, acc = lax.fori_loop(0, i, body, (m0, l0, acc0))
        outs.append((acc / l, in_head))
    o_ref[0] = jnp.where(outs[0][1], outs[0][0], outs[1][0]).astype(o_ref.dtype)


def _moba_attention(q, k, v, bias):
    b, s, w = q.shape
    nbk = s // MOBA_BLOCK
    pair_w = 2 * HEAD_DIM
    npair = w // pair_w
    return pl.pallas_call(
        functools.partial(_moba_kernel, nbk=nbk),
        grid=(b, npair, nbk),
        in_specs=[pl.BlockSpec((1, MOBA_BLOCK, pair_w), lambda bi, p, i: (bi, i, p)),
                  pl.BlockSpec((1, s, pair_w), lambda bi, p, i: (bi, 0, p)),
                  pl.BlockSpec((1, s, pair_w), lambda bi, p, i: (bi, 0, p)),
                  pl.BlockSpec((2, MOBA_BIAS_TILES, MOBA_BLOCK, MOBA_BLOCK), lambda bi, p, i: (p, 0, 0, 0))],
        out_specs=pl.BlockSpec((1, MOBA_BLOCK, pair_w), lambda bi, p, i: (bi, i, p)),
        out_shape=jax.ShapeDtypeStruct((b, s, w), BF16),
        scratch_shapes=[pltpu.VMEM((nbk, pair_w), F32)],
        compiler_params=_params(("parallel", "parallel", "arbitrary")),
        name="moba_attn",
    )(q, k, v, bias)


def _sb_kernel(q_ref, k_ref, v_ref, o_ref):
    i = pl.program_id(2)
    blk = SB_BLOCK
    q = q_ref[0]
    row = lax.broadcasted_iota(jnp.int32, (blk, blk), 0)
    colb = lax.broadcasted_iota(jnp.int32, (blk, blk), 1)
    strictly_past = colb < row
    r2 = lax.broadcasted_iota(jnp.int32, (blk, 2 * blk), 0)
    c2 = lax.broadcasted_iota(jnp.int32, (blk, 2 * blk), 1)
    tri_ones = jnp.where((c2 >= blk) | (r2 >= c2), 1.0, 0.0).astype(BF16)

    def block(j, carry, acc, diagonal):
        off = pl.multiple_of(j * blk, blk)
        kj = k_ref[0, pl.ds(off, blk), :]
        vj = v_ref[0, pl.ds(off, blk), :]
        z = _dot_nt(q, kj) * (SB_HEAD_DIM ** -0.5)
        lk = -(jnp.maximum(z, 0.0) + jnp.log1p(jnp.exp(-jnp.abs(z))))
        if diagonal:
            lk = jnp.where(strictly_past, lk, 0.0)
        sums = _dot(lk.astype(BF16), tri_ones)
        log_a = sums[:, :blk] + carry + z
        if diagonal:
            log_a = jnp.where(strictly_past, log_a, NEG_INF)
        a = jnp.exp(log_a)
        acc = acc + _dot(a.astype(BF16), vj)
        return carry + sums[:, blk:], acc

    carry, acc = block(i, jnp.zeros((blk, blk), F32), jnp.zeros((blk, SB_HEAD_DIM), F32), True)

    def body(t, c):
        return block(i - 1 - t, c[0], c[1], False)

    _, acc = lax.fori_loop(0, i, body, (carry, acc))
    o_ref[0] = acc.astype(o_ref.dtype)


def _sb_attention(q, k, v):
    b, s, w = q.shape
    nq = s // SB_BLOCK
    return pl.pallas_call(
        _sb_kernel,
        grid=(b, SB_HEADS, nq),
        in_specs=[pl.BlockSpec((1, SB_BLOCK, SB_HEAD_DIM), lambda bi, h, i: (bi, i, h)),
                  pl.BlockSpec((1, s, SB_HEAD_DIM), lambda bi, h, i: (bi, 0, h)),
                  pl.BlockSpec((1, s, SB_HEAD_DIM), lambda bi, h, i: (bi, 0, h))],
        out_specs=pl.BlockSpec((1, SB_BLOCK, SB_HEAD_DIM), lambda bi, h, i: (bi, i, h)),
        out_shape=jax.ShapeDtypeStruct((b, s, w), BF16),
        compiler_params=_params(("parallel", "parallel", "arbitrary")),
        name="stick_breaking_attn",
    )(q, k, v)


def _merge_kernel(x_ref, oa_ref, ob_ref, oc_ref, od_ref, g0, g1, g2, g3, w0, w1, w2, w3, out_ref):
    x = x_ref[...]
    acc = None
    for o_ref, wg_ref, wo_ref in ((oa_ref, g0, w0), (ob_ref, g1, w1), (oc_ref, g2, w2), (od_ref, g3, w3)):
        gate = jax.nn.sigmoid(_dot(x, wg_ref[...]))
        term = gate * _dot(o_ref[...], wo_ref[...])
        acc = term if acc is None else acc + term
    out_ref[...] = acc.astype(out_ref.dtype)


def _merge(x_bf, branches, w_gates, w_o, tm=512, tn=512):
    n, d = x_bf.shape
    nj = d // tn
    row = lambda j, i: (i, 0)
    in_specs = [pl.BlockSpec((tm, d), row)] + [pl.BlockSpec((tm, BRANCH_WIDTH), row)] * N_BRANCHES
    in_specs += [pl.BlockSpec((d, tn), functools.partial(lambda j, i, br: (0, br * nj + j), br=br))
                 for br in range(N_BRANCHES)]
    in_specs += [pl.BlockSpec((BRANCH_WIDTH, tn), functools.partial(lambda j, i, br: (br, j), br=br))
                 for br in range(N_BRANCHES)]
    return pl.pallas_call(
        _merge_kernel,
        grid=(nj, n // tm),
        in_specs=in_specs,
        out_specs=pl.BlockSpec((tm, tn), lambda j, i: (i, j)),
        out_shape=jax.ShapeDtypeStruct((n, d), BF16),
        compiler_params=_params(("parallel", "arbitrary")),
        name="gated_merge",
    )(x_bf, *branches, *([w_gates] * N_BRANCHES), *([w_o] * N_BRANCHES))


def _layer_norm(r, g, b):
    mu = r.mean(-1, keepdims=True)
    var = jnp.square(r - mu).mean(-1, keepdims=True)
    return (r - mu) * lax.rsqrt(var + LN_EPS) * g + b


def _out_ln_kernel(m_ref, w_ref, x_ref, g_ref, b_ref, wr_ref, xo_ref, xb_ref, lg_ref):
    y = _dot(m_ref[...], w_ref[...])
    xn = _layer_norm(ALPHA * x_ref[...] + y, g_ref[...], b_ref[...])
    xo_ref[...] = xn
    xb = xn.astype(BF16)
    xb_ref[...] = xb
    lg_ref[...] = _dot(xb, wr_ref[...])


def _out_ln(merged, w_out, x, g, b, w_router, tm=512):
    n, d = x.shape
    e = w_router.shape[1]
    row = lambda i: (i, 0)
    fix = lambda i: (0, 0)
    return pl.pallas_call(
        _out_ln_kernel,
        grid=(n // tm,),
        in_specs=[pl.BlockSpec((tm, d), row), pl.BlockSpec((d, d), fix), pl.BlockSpec((tm, d), row),
                  pl.BlockSpec((1, d), fix), pl.BlockSpec((1, d), fix), pl.BlockSpec((d, e), fix)],
        out_specs=(pl.BlockSpec((tm, d), row), pl.BlockSpec((tm, d), row), pl.BlockSpec((tm, e), row)),
        out_shape=(jax.ShapeDtypeStruct((n, d), F32), jax.ShapeDtypeStruct((n, d), BF16),
                   jax.ShapeDtypeStruct((n, e), F32)),
        compiler_params=_params(("parallel",)),
        name="out_proj_ln_router",
    )(merged, w_out, x, g, b, w_router)


def _expert_kernel(be_ref, used_ref, xs_ref, sw_ref, wg_ref, wu_ref, wd_ref, y_ref):
    @pl.when(used_ref[pl.program_id(0)] > 0)
    def _():
        xs = xs_ref[...]
        h = jax.nn.silu(_dot(xs, wg_ref[0])) * _dot(xs, wu_ref[0])
        y = _dot(h.astype(BF16), wd_ref[0])
        y_ref[...] = (y * sw_ref[...]).astype(y_ref.dtype)


def _experts(xs, slot_w, blk_expert, blk_used, wg, wu, wd, tb):
    ns, d = xs.shape
    hdn = wg.shape[2]
    nb = ns // tb
    grid_spec = pltpu.PrefetchScalarGridSpec(
        num_scalar_prefetch=2, grid=(nb,),
        in_specs=[pl.BlockSpec((tb, d), lambda i, be, us: (i, 0)),
                  pl.BlockSpec((tb, 1), lambda i, be, us: (i, 0)),
                  pl.BlockSpec((1, d, hdn), lambda i, be, us: (be[i], 0, 0)),
                  pl.BlockSpec((1, d, hdn), lambda i, be, us: (be[i], 0, 0)),
                  pl.BlockSpec((1, hdn, d), lambda i, be, us: (be[i], 0, 0))],
        out_specs=pl.BlockSpec((tb, d), lambda i, be, us: (i, 0)))
    return pl.pallas_call(
        _expert_kernel, grid_spec=grid_spec,
        out_shape=jax.ShapeDtypeStruct((ns, d), BF16),
        compiler_params=_params(("arbitrary",)),
        name="grouped_swiglu",
    )(blk_expert, blk_used, xs, slot_w, wg, wu, wd)


def _final_ln_kernel(x_ref, sh_ref, ys_ref, g_ref, b_ref, xo_ref, xb_ref):
    ffn = sh_ref[...].astype(F32)
    for kk in range(TOP_K):
        ffn = ffn + ys_ref[kk].astype(F32)
    xn = _layer_norm(ALPHA * x_ref[...] + ffn, g_ref[...], b_ref[...])
    xo_ref[...] = xn
    xb_ref[...] = xn.astype(BF16)


def _final_ln(x, shared, ys_k, g, b, tm=256):
    n, d = x.shape
    row = lambda i: (i, 0)
    fix = lambda i: (0, 0)
    return pl.pallas_call(
        _final_ln_kernel,
        grid=(n // tm,),
        in_specs=[pl.BlockSpec((tm, d), row), pl.BlockSpec((tm, d), row),
                  pl.BlockSpec((TOP_K, tm, d), lambda i: (0, i, 0)),
                  pl.BlockSpec((1, d), fix), pl.BlockSpec((1, d), fix)],
        out_specs=(pl.BlockSpec((tm, d), row), pl.BlockSpec((tm, d), row)),
        out_shape=(jax.ShapeDtypeStruct((n, d), F32), jax.ShapeDtypeStruct((n, d), BF16)),
        compiler_params=_params(("parallel",)),
        name="moe_combine_ln",
    )(x, shared, ys_k, g, b)


def _route(logits, router_bias):
    n = logits.shape[0]
    scores = jax.nn.sigmoid(logits)
    biased = scores + router_bias.astype(F32)
    group_score = lax.top_k(biased.reshape(n, N_EXPERT_GROUPS, -1), 2)[0].sum(-1)
    _, top_groups = lax.top_k(group_score, TOPK_GROUPS)
    group_mask = (top_groups[..., None] == jnp.arange(N_EXPERT_GROUPS)).any(axis=1)
    masked = jnp.where(jnp.repeat(group_mask, N_EXPERTS // N_EXPERT_GROUPS, axis=1), biased, NEG_INF)
    _, idx = lax.top_k(masked, TOP_K)
    gw = jnp.take_along_axis(scores, idx, axis=1)
    return idx, gw / gw.sum(-1, keepdims=True) * ROUTED_SCALE


def _dispatch(idx, gw, tb):
    n = idx.shape[0]
    nk = n * TOP_K
    e_flat = idx.reshape(-1)
    t_flat = jnp.arange(nk, dtype=jnp.int32) // TOP_K
    order = jnp.argsort(e_flat)
    e_s, t_s = e_flat[order], t_flat[order]
    counts = jnp.bincount(e_flat, length=N_EXPERTS)
    starts = jnp.cumsum(counts) - counts
    pcounts = (counts + tb - 1) // tb * tb
    pends = jnp.cumsum(pcounts)
    pstarts = pends - pcounts
    dest_sorted = (pstarts[e_s] + (jnp.arange(nk, dtype=jnp.int32) - starts[e_s])).astype(jnp.int32)
    nb = -(-nk // tb) + N_EXPERTS
    tok = jnp.zeros((nb * tb,), jnp.int32).at[dest_sorted].set(t_s)
    slot_w = jnp.zeros((nb * tb,), F32).at[dest_sorted].set(gw.reshape(-1)[order])
    dest = jnp.zeros((nk,), jnp.int32).at[order].set(dest_sorted)
    blk_start = jnp.arange(nb, dtype=jnp.int32) * tb
    blk_expert = jnp.minimum(jnp.searchsorted(pends, blk_start, side='right'), N_EXPERTS - 1).astype(jnp.int32)
    blk_used = (blk_start < pends[-1]).astype(jnp.int32)
    return tok, slot_w, dest, blk_expert, blk_used


def _layer(x, x_bf, biases, w_in, w_o, w_out, sinks, ln1_g, ln1_b, w_router, router_bias,
           w_gate, w_up, w_down, w_sh_gate, w_sh_up, w_sh_down, ln2_g, ln2_b, b, s):
    n, d = x.shape
    bias_a, bias_b, bias_d = biases
    w_qkv = w_in[:, :QKV_WIDTH].astype(BF16)
    w_gates = w_in[:, QKV_WIDTH:].astype(BF16)
    proj = _matmul(x_bf, w_qkv, BF16, tm=min(1024, n), tn=768)
    bw = BRANCH_WIDTH
    part = lambda i, w=bw: proj[:, i * bw:i * bw + w].reshape(b, s, w)
    qa, ka, va, qb, kb, vb, qc, kc, vc, qd = (part(i) for i in range(10))
    kd = part(10, KV_W)
    vd = proj[:, 10 * bw + KV_W:10 * bw + 2 * KV_W].reshape(b, s, KV_W)

    parts = []
    for dil, ba in zip(A_DILATIONS, bias_a):
        o, lse = _band_attention(qa, ka, va, ba, dil)
        parts += [o.reshape(n, bw), lse.reshape(n, bw)]
    o_a = _mix_combine(parts)
    o_b = _moba_attention(qb, kb, vb, bias_b).reshape(n, bw)
    o_c = _sb_attention(qc, kc, vc).reshape(n, bw)
    sink_row = jnp.repeat(sinks.astype(F32), HEAD_DIM)[None, :]
    o_d = _band_attention(qd, kd, vd, bias_d, 1, sink_row).reshape(n, bw)

    merged = _merge(x_bf, (o_a, o_b, o_c, o_d), w_gates, w_o.astype(BF16))
    x1, x1_bf, logits = _out_ln(merged, w_out.astype(BF16), x, ln1_g[None, :], ln1_b[None, :],
                                w_router.astype(BF16))

    idx, gw = _route(logits, router_bias)
    tb = EXPERT_BLOCK
    tok, slot_w, dest, blk_expert, blk_used = _dispatch(idx, gw, tb)
    xs = x1_bf[tok]
    ys = _experts(xs, slot_w[:, None], blk_expert, blk_used,
                  w_gate.astype(BF16), w_up.astype(BF16), w_down.astype(BF16), tb)
    nsh = n // tb
    shared = _experts(x1_bf, jnp.ones((n, 1), F32), jnp.zeros((nsh,), jnp.int32), jnp.ones((nsh,), jnp.int32),
                      w_sh_gate.astype(BF16)[None], w_sh_up.astype(BF16)[None], w_sh_down.astype(BF16)[None], tb)
    ys_k = ys[dest.reshape(n, TOP_K).T]
    return _final_ln(x1, shared, ys_k, ln2_g[None, :], ln2_b[None, :])


def kernel(x, rel_bias, w_in, w_o, w_out, attn_sinks, ln1_g, ln1_b, w_router, router_bias,
           w_gate, w_up, w_down, w_sh_gate, w_sh_up, w_sh_down, ln2_g, ln2_b):
    b, s, d = x.shape
    n = b * s
    h = MIX_HEADS
    bias_a = [_band_bias(rel_bias[:, :h], win // dil, dil) for win, dil in zip(A_WINDOWS, A_DILATIONS)]
    bias_b = _moba_bias(rel_bias[:, h:2 * h])
    bias_d = _band_bias(rel_bias[:, 2 * h:], SWA_WINDOW - 1, 1)
    xf = x.reshape(n, d)
    xb = xf.astype(BF16)
    for l in range(w_in.shape[0]):
        xf, xb = _layer(xf, xb, (bias_a, bias_b, bias_d), w_in[l], w_o[l], w_out[l], attn_sinks[l],
                        ln1_g[l], ln1_b[l], w_router[l], router_bias[l], w_gate[l], w_up[l], w_down[l],
                        w_sh_gate[l], w_sh_up[l], w_sh_down[l], ln2_g[l], ln2_b[l], b, s)
    return xf.reshape(b, s, d)
```

```python
import functools
import math

import numpy as np
import jax
import jax.numpy as jnp
from jax import lax
from jax.experimental import pallas as pl
from jax.experimental.pallas import tpu as pltpu

D_MODEL = 2048
DEPTH = 4
HEAD_DIM = 64
N_BRANCHES = 4
BRANCH_WIDTH = D_MODEL // N_BRANCHES
MIX_HEADS = BRANCH_WIDTH // HEAD_DIM
SB_HEADS = 4
SB_HEAD_DIM = BRANCH_WIDTH // SB_HEADS
A_WINDOWS = (128, 512, 2048)
A_DILATIONS = (1, 4, 16)
MOBA_BLOCK = 256
MOBA_TOPK = 3
SB_BLOCK = 128
SWA_WINDOW = 128
D_KV_HEADS = 2
BAND_BLOCK = 128
REL_BUCKETS = 32
REL_MAX_EXACT = 16
REL_MAX_DIST = 2048
N_EXPERTS = 64
TOP_K = 8
N_EXPERT_GROUPS = 8
TOPK_GROUPS = 4
EXPERT_HIDDEN = D_MODEL * 3 // 16
ROUTED_SCALE = 2.5
EXPERT_BLOCK = 512
LN_EPS = 1e-5
ALPHA = (2 * DEPTH) ** 0.25
NEG_INF = -1e30
KV_W = D_KV_HEADS * HEAD_DIM
QKV_WIDTH = 10 * BRANCH_WIDTH + 2 * KV_W
PROJ_WIDTH = 11 * BRANCH_WIDTH
MOBA_BIAS_TILES = 8
QA, KA, VA, QB, KB, VB, QC, KC, VC, QD = range(10)
KD_COL = 10 * BRANCH_WIDTH // KV_W
VD_COL = KD_COL + 1
F32_EXP_ZERO_BELOW = -104.0

VMEM_LIMIT_BYTES = 56 * 1024 * 1024

F32 = jnp.float32
BF16 = jnp.bfloat16


def _params(semantics, vmem=VMEM_LIMIT_BYTES):
    return pltpu.CompilerParams(dimension_semantics=semantics, vmem_limit_bytes=vmem)


def _dot(a, b):
    return jnp.dot(a, b, preferred_element_type=F32)


def _dot_nt(a, b):
    return lax.dot_general(a, b, (((1,), (1,)), ((), ())), preferred_element_type=F32)


def _mm_kernel(x_ref, w_ref, o_ref):
    o_ref[...] = _dot(x_ref[...], w_ref[...]).astype(o_ref.dtype)


def _matmul(x, w, out_dtype, tm, tn):
    m, k = x.shape
    n = w.shape[1]
    return pl.pallas_call(
        _mm_kernel,
        grid=(m // tm, n // tn),
        in_specs=[pl.BlockSpec((tm, k), lambda i, j: (i, 0)),
                  pl.BlockSpec((k, tn), lambda i, j: (0, j))],
        out_specs=pl.BlockSpec((tm, tn), lambda i, j: (i, j)),
        out_shape=jax.ShapeDtypeStruct((m, n), out_dtype),
        compiler_params=_params(("parallel", "arbitrary")),
        name="qkv_proj",
    )(x, w)


def _t5_bucket(dist):
    n = jnp.maximum(dist, 0)
    nf = jnp.maximum(n, 1).astype(F32)
    large = REL_MAX_EXACT + (jnp.log(nf / REL_MAX_EXACT) / math.log(REL_MAX_DIST / REL_MAX_EXACT)
                             * (REL_BUCKETS - REL_MAX_EXACT)).astype(jnp.int32)
    return jnp.where(n < REL_MAX_EXACT, n, jnp.minimum(large, REL_BUCKETS - 1))


def _band_bias(tab, max_steps, dist_scale):
    qi = np.arange(BAND_BLOCK)[:, None]
    kj = np.arange(2 * BAND_BLOCK)[None, :]
    dist = BAND_BLOCK + qi - kj
    band = (dist >= 0) & (dist <= max_steps)
    bias = tab[_t5_bucket(jnp.asarray(dist * dist_scale, jnp.int32))].astype(F32)
    bias = jnp.where(jnp.asarray(band)[:, :, None], bias, NEG_INF)
    return bias.transpose(2, 0, 1)


def _moba_bias(tab):
    r = np.arange(MOBA_BLOCK)[:, None]
    c = np.arange(MOBA_BLOCK)[None, :]
    dist = np.stack([dl * MOBA_BLOCK + r - c for dl in range(MOBA_BIAS_TILES)])
    bias = tab[_t5_bucket(jnp.asarray(dist, jnp.int32))].astype(F32)
    bias = jnp.where(jnp.asarray(dist >= 0)[..., None], bias, NEG_INF)
    return bias.transpose(3, 0, 1, 2)


def _band_kernel(*refs, kv_rep, with_sink):
    if with_sink:
        q_ref, kp_ref, ko_ref, vp_ref, vo_ref, bias_ref, sink_ref, o_ref = refs
    else:
        q_ref, kp_ref, ko_ref, vp_ref, vo_ref, bias_ref, o_ref, lse_ref = refs
    has_prev = pl.program_id(2) > 0
    q = q_ref[0]
    kp, ko, vp, vo = kp_ref[0], ko_ref[0], vp_ref[0], vo_ref[0]
    for h in range(MIX_HEADS):
        g = h // kv_rep
        hs = slice(h * HEAD_DIM, (h + 1) * HEAD_DIM)
        gs = slice(g * HEAD_DIM, (g + 1) * HEAD_DIM)
        qh = q[:, hs]
        bias = bias_ref[h]
        s_p = _dot_nt(qh, kp[:, gs]) * (HEAD_DIM ** -0.5) + bias[:, :BAND_BLOCK]
        s_o = _dot_nt(qh, ko[:, gs]) * (HEAD_DIM ** -0.5) + bias[:, BAND_BLOCK:]
        s_p = jnp.where(has_prev, s_p, NEG_INF)
        m = jnp.maximum(s_p.max(-1, keepdims=True), s_o.max(-1, keepdims=True))
        p_p = jnp.exp(s_p - m)
        p_o = jnp.exp(s_o - m)
        den = p_p.sum(-1, keepdims=True) + p_o.sum(-1, keepdims=True)
        o = (_dot(p_p.astype(BF16), vp[:, gs]) + _dot(p_o.astype(BF16), vo[:, gs])) / den
        lse = m + jnp.log(den)
        if with_sink:
            keep = jax.nn.sigmoid(lse - sink_ref[:, hs])
            o_ref[0, :, hs] = (o * keep).astype(o_ref.dtype)
        else:
            o_ref[0, :, hs] = o.astype(o_ref.dtype)
            lse_ref[0, :, hs] = jnp.broadcast_to(lse, (BAND_BLOCK, HEAD_DIM))


def _band_attention(proj, qcol, kcol, vcol, kvw, bias, dil, sink_row=None):
    b, s, pw = proj.shape
    qw = BRANCH_WIDTH
    l = s // dil
    nb = l // BAND_BLOCK
    kv_rep = qw // kvw
    pv = proj.reshape(b, l, dil * pw)
    qpr, kpr = pw // qw, pw // kvw
    own = lambda bi, r, n: (bi, n, r)
    q_own = lambda bi, r, n: (bi, n, r * qpr + qcol)
    k_own = lambda bi, r, n: (bi, n, r * kpr + kcol)
    k_prev = lambda bi, r, n: (bi, jnp.maximum(n - 1, 0), r * kpr + kcol)
    v_own = lambda bi, r, n: (bi, n, r * kpr + vcol)
    v_prev = lambda bi, r, n: (bi, jnp.maximum(n - 1, 0), r * kpr + vcol)
    in_specs = [pl.BlockSpec((1, BAND_BLOCK, qw), q_own),
                pl.BlockSpec((1, BAND_BLOCK, kvw), k_prev),
                pl.BlockSpec((1, BAND_BLOCK, kvw), k_own),
                pl.BlockSpec((1, BAND_BLOCK, kvw), v_prev),
                pl.BlockSpec((1, BAND_BLOCK, kvw), v_own),
                pl.BlockSpec((MIX_HEADS, BAND_BLOCK, 2 * BAND_BLOCK), lambda bi, r, n: (0, 0, 0))]
    args = [pv, pv, pv, pv, pv, bias]
    with_sink = sink_row is not None
    if with_sink:
        in_specs.append(pl.BlockSpec((1, qw), lambda bi, r, n: (0, 0)))
        args.append(sink_row)
        out_shape = jax.ShapeDtypeStruct((b, l, dil * qw), BF16)
        out_specs = pl.BlockSpec((1, BAND_BLOCK, qw), own)
    else:
        out_shape = (jax.ShapeDtypeStruct((b, l, dil * qw), F32),) * 2
        out_specs = (pl.BlockSpec((1, BAND_BLOCK, qw), own),) * 2
    out = pl.pallas_call(
        functools.partial(_band_kernel, kv_rep=kv_rep, with_sink=with_sink),
        grid=(b, dil, nb),
        in_specs=in_specs, out_specs=out_specs, out_shape=out_shape,
        compiler_params=_params(("parallel", "parallel", "arbitrary")),
        name="band_attn_sink" if with_sink else f"band_attn_d{dil}",
    )(*args)
    if with_sink:
        return out.reshape(b, s, qw)
    return out[0].reshape(b, s, qw), out[1].reshape(b, s, qw)


def _mix_combine_kernel(o1, l1, o2, l2, o3, l3, out_ref):
    a, bq, c = l1[...], l2[...], l3[...]
    m = jnp.maximum(jnp.maximum(a, bq), c)
    ea, eb, ec = jnp.exp(a - m), jnp.exp(bq - m), jnp.exp(c - m)
    den = ea + eb + ec
    out_ref[...] = (ea / den * o1[...] + eb / den * o2[...] + ec / den * o3[...]).astype(out_ref.dtype)


def _mix_combine(parts, tm=512):
    n, w = parts[0].shape
    spec = pl.BlockSpec((tm, w), lambda i: (i, 0))
    return pl.pallas_call(
        _mix_combine_kernel, grid=(n // tm,),
        in_specs=[spec] * 6, out_specs=spec,
        out_shape=jax.ShapeDtypeStruct((n, w), BF16),
        compiler_params=_params(("parallel",)),
        name="dilated_combine",
    )(*parts)


def _moba_kernel(q_ref, k_ref, v_ref, bias_ref, o_ref, kmean_ref, *, nbk):
    i = pl.program_id(2)
    blk = MOBA_BLOCK
    pair_w = 2 * HEAD_DIM

    @pl.when(i == 0)
    def _():
        kk = k_ref[0].astype(F32).reshape(nbk, blk, pair_w)
        kmean_ref[...] = kk.sum(axis=1) * (1.0 / blk)

    q = q_ref[0]
    lane = lax.broadcasted_iota(jnp.int32, (1, pair_w), 1)
    zero = jnp.zeros_like(q)
    qs = jnp.concatenate([jnp.where(lane < HEAD_DIM, q, zero), jnp.where(lane >= HEAD_DIM, q, zero)], axis=0)
    qsc = qs * (HEAD_DIM ** -0.5)
    col = lax.broadcasted_iota(jnp.int32, (2 * blk, nbk), 1)
    gate = jnp.where(col < i, _dot_nt(qs, kmean_ref[...].astype(BF16)), NEG_INF)
    sel = jnp.zeros((2 * blk, nbk), F32)
    for _ in range(MOBA_TOPK):
        mx = gate.max(-1, keepdims=True)
        first = jnp.where(gate == mx, col, nbk).min(-1, keepdims=True)
        pick = col == first
        sel = jnp.where(pick & (col < i), 1.0, sel)
        gate = jnp.where(pick, -jnp.inf, gate)

    def bias_tile(dl):
        dl = jnp.clip(dl, 0, MOBA_BIAS_TILES - 1)
        return jnp.concatenate([bias_ref[0, dl], bias_ref[1, dl]], axis=0)

    own = pl.ds(pl.multiple_of(i * blk, blk), blk)
    s = _dot_nt(qsc, k_ref[0, own, :]) + bias_tile(0)
    m0 = s.max(-1, keepdims=True)
    p = jnp.exp(s - m0)
    l0 = p.sum(-1, keepdims=True)
    acc0 = _dot(p.astype(BF16), v_ref[0, own, :])
    in_a = lax.broadcasted_iota(jnp.int32, (1, 2 * blk), 1) < blk

    def body(c, carry):
        m, l, acc = carry
        ja = 2 * c
        rows = pl.ds(pl.multiple_of(ja * blk, 2 * blk), 2 * blk)
        chosen_a = jnp.where(col == ja, sel, 0.0).sum(-1, keepdims=True)
        chosen_b = jnp.where(col == ja + 1, sel, 0.0).sum(-1, keepdims=True)
        bias = jnp.concatenate([bias_tile(i - ja), bias_tile(i - ja - 1)], axis=1)
        s = _dot_nt(qsc, k_ref[0, rows, :]) + bias
        s = jnp.where(jnp.where(in_a, chosen_a, chosen_b) > 0.0, s, NEG_INF)
        m_new = jnp.maximum(m, s.max(-1, keepdims=True))
        scale = jnp.exp(m - m_new)
        p = jnp.exp(s - m_new)
        l = scale * l + p.sum(-1, keepdims=True)
        acc = scale * acc + _dot(p.astype(BF16), v_ref[0, rows, :])
        return m_new, l, acc

    _, l, acc = lax.fori_loop(0, (i + 1) // 2, body, (m0, l0, acc0))
    o = acc / l
    o_ref[0] = jnp.where(lane < HEAD_DIM, o[:blk], o[blk:]).astype(o_ref.dtype)


def _moba_attention(proj, bias):
    b, s, pw = proj.shape
    nbk = s // MOBA_BLOCK
    pair_w = 2 * HEAD_DIM
    npair = BRANCH_WIDTH // pair_w
    slab = BRANCH_WIDTH // pair_w
    return pl.pallas_call(
        functools.partial(_moba_kernel, nbk=nbk),
        grid=(b, npair, nbk),
        in_specs=[pl.BlockSpec((1, MOBA_BLOCK, pair_w), lambda bi, p, i: (bi, i, QB * slab + p)),
                  pl.BlockSpec((1, s, pair_w), lambda bi, p, i: (bi, 0, KB * slab + p)),
                  pl.BlockSpec((1, s, pair_w), lambda bi, p, i: (bi, 0, VB * slab + p)),
                  pl.BlockSpec((2, MOBA_BIAS_TILES, MOBA_BLOCK, MOBA_BLOCK), lambda bi, p, i: (p, 0, 0, 0))],
        out_specs=pl.BlockSpec((1, MOBA_BLOCK, pair_w), lambda bi, p, i: (bi, i, p)),
        out_shape=jax.ShapeDtypeStruct((b, s, BRANCH_WIDTH), BF16),
        scratch_shapes=[pltpu.VMEM((nbk, pair_w), F32)],
        compiler_params=_params(("parallel", "parallel", "arbitrary")),
        name="moba_attn",
    )(proj, proj, proj, bias)


def _sb_kernel(q_ref, k_ref, v_ref, o_ref, kmax_ref):
    i = pl.program_id(1)
    blk = SB_BLOCK
    hd = SB_HEAD_DIM
    s_len = k_ref.shape[1]
    scale = hd ** -0.5

    @pl.when(i == 0)
    def _():
        chunk = 512
        for h in range(SB_HEADS):
            def norm_max(c, mx):
                kk = k_ref[0, pl.ds(pl.multiple_of(c * chunk, chunk), chunk), h * hd:(h + 1) * hd].astype(F32)
                return jnp.maximum(mx, (kk * kk).sum(-1, keepdims=True).max(0, keepdims=True))
            mx = lax.fori_loop(0, s_len // chunk, norm_max, jnp.zeros((1, 1), F32))
            kmax_ref[h] = jnp.broadcast_to(jnp.sqrt(mx), kmax_ref.shape[1:])

    row = lax.broadcasted_iota(jnp.int32, (blk, blk), 0)
    colb = lax.broadcasted_iota(jnp.int32, (blk, blk), 1)
    strictly_past = colb < row
    r2 = lax.broadcasted_iota(jnp.int32, (blk, 2 * blk), 0)
    c2 = lax.broadcasted_iota(jnp.int32, (blk, 2 * blk), 1)
    tri_ones = jnp.where((c2 >= blk) | (r2 >= c2), 1.0, 0.0).astype(BF16)
    qs = [q_ref[0, :, h * hd:(h + 1) * hd] for h in range(SB_HEADS)]
    z_bound = []
    for h in range(SB_HEADS):
        qf = qs[h].astype(F32)
        z_bound.append(jnp.sqrt((qf * qf).sum(-1, keepdims=True)) * kmax_ref[h][0:1, 0:1] * scale)

    def block(h, j, carry, acc, diagonal):
        rows = pl.ds(pl.multiple_of(j * blk, blk), blk)
        kj = k_ref[0, rows, h * hd:(h + 1) * hd]
        vj = v_ref[0, rows, h * hd:(h + 1) * hd]
        z = _dot_nt(qs[h], kj) * scale
        lk = -(jnp.maximum(z, 0.0) + jnp.log1p(jnp.exp(-jnp.abs(z))))
        if diagonal:
            lk = jnp.where(strictly_past, lk, 0.0)
        sums = _dot(lk.astype(BF16), tri_ones)
        log_a = sums[:, :blk] + carry + z
        if diagonal:
            log_a = jnp.where(strictly_past, log_a, NEG_INF)
        acc = acc + _dot(jnp.exp(log_a).astype(BF16), vj)
        return carry + sums[:, blk:], acc

    def all_dead(carries):
        worst = None
        for h in range(SB_HEADS):
            w = (carries[h] * 0.98 + z_bound[h] * 1.02).max()
            worst = w if worst is None else jnp.maximum(worst, w)
        return worst < F32_EXP_ZERO_BELOW - 6.0

    state = [block(h, i, jnp.zeros((blk, blk), F32), jnp.zeros((blk, hd), F32), True) for h in range(SB_HEADS)]
    carries = tuple(st[0] for st in state)
    accs = tuple(st[1] for st in state)

    def cond(c):
        t, dead, _, _ = c
        return (t < i) & jnp.logical_not(dead)

    def body(c):
        t, _, carries, accs = c
        new = [block(h, i - 1 - t, carries[h], accs[h], False) for h in range(SB_HEADS)]
        carries = tuple(st[0] for st in new)
        return t + 1, all_dead(carries), carries, tuple(st[1] for st in new)

    _, _, _, accs = lax.while_loop(cond, body, (jnp.int32(0), all_dead(carries), carries, accs))
    for h in range(SB_HEADS):
        o_ref[0, :, h * hd:(h + 1) * hd] = accs[h].astype(o_ref.dtype)


def _sb_attention(proj):
    b, s, pw = proj.shape
    nq = s // SB_BLOCK
    w = BRANCH_WIDTH
    return pl.pallas_call(
        _sb_kernel,
        grid=(b, nq),
        in_specs=[pl.BlockSpec((1, SB_BLOCK, w), lambda bi, i: (bi, i, QC)),
                  pl.BlockSpec((1, s, w), lambda bi, i: (bi, 0, KC)),
                  pl.BlockSpec((1, s, w), lambda bi, i: (bi, 0, VC))],
        out_specs=pl.BlockSpec((1, SB_BLOCK, w), lambda bi, i: (bi, i, 0)),
        out_shape=jax.ShapeDtypeStruct((b, s, w), BF16),
        scratch_shapes=[pltpu.VMEM((SB_HEADS, 8, 128), F32)],
        compiler_params=_params(("parallel", "arbitrary")),
        name="stick_breaking_attn",
    )(proj, proj, proj)


def _merge_kernel(x_ref, oa_ref, ob_ref, oc_ref, od_ref, g0, g1, g2, g3, w0, w1, w2, w3, out_ref):
    x = x_ref[...]
    acc = None
    for o_ref, wg_ref, wo_ref in ((oa_ref, g0, w0), (ob_ref, g1, w1), (oc_ref, g2, w2), (od_ref, g3, w3)):
        gate = jax.nn.sigmoid(_dot(x, wg_ref[...]))
        term = gate * _dot(o_ref[...], wo_ref[...])
        acc = term if acc is None else acc + term
    out_ref[...] = acc.astype(out_ref.dtype)


def _merge(x_bf, branches, w_gates, w_o, tm=512, tn=512):
    n, d = x_bf.shape
    nj = d // tn
    row = lambda j, i: (i, 0)
    in_specs = [pl.BlockSpec((tm, d), row)] + [pl.BlockSpec((tm, BRANCH_WIDTH), row)] * N_BRANCHES
    in_specs += [pl.BlockSpec((d, tn), functools.partial(lambda j, i, br: (0, br * nj + j), br=br))
                 for br in range(N_BRANCHES)]
    in_specs += [pl.BlockSpec((BRANCH_WIDTH, tn), functools.partial(lambda j, i, br: (br, j), br=br))
                 for br in range(N_BRANCHES)]
    return pl.pallas_call(
        _merge_kernel,
        grid=(nj, n // tm),
        in_specs=in_specs,
        out_specs=pl.BlockSpec((tm, tn), lambda j, i: (i, j)),
        out_shape=jax.ShapeDtypeStruct((n, d), BF16),
        compiler_params=_params(("parallel", "arbitrary")),
        name="gated_merge",
    )(x_bf, *branches, *([w_gates] * N_BRANCHES), *([w_o] * N_BRANCHES))


def _layer_norm(r, g, b):
    mu = r.mean(-1, keepdims=True)
    var = jnp.square(r - mu).mean(-1, keepdims=True)
    return (r - mu) * lax.rsqrt(var + LN_EPS) * g + b


def _out_ln_kernel(m_ref, w_ref, x_ref, g_ref, b_ref, wrt_ref, xo_ref, xb_ref, lgt_ref):
    y = _dot(m_ref[...], w_ref[...])
    xn = _layer_norm(ALPHA * x_ref[...] + y, g_ref[...], b_ref[...])
    xo_ref[...] = xn
    xb = xn.astype(BF16)
    xb_ref[...] = xb
    lgt_ref[...] = _dot_nt(wrt_ref[...], xb)


def _out_ln(merged, w_out, x, g, b, w_router_t, tm=512):
    n, d = x.shape
    e = w_router_t.shape[0]
    row = lambda i: (i, 0)
    fix = lambda i: (0, 0)
    return pl.pallas_call(
        _out_ln_kernel,
        grid=(n // tm,),
        in_specs=[pl.BlockSpec((tm, d), row), pl.BlockSpec((d, d), fix), pl.BlockSpec((tm, d), row),
                  pl.BlockSpec((1, d), fix), pl.BlockSpec((1, d), fix), pl.BlockSpec((e, d), fix)],
        out_specs=(pl.BlockSpec((tm, d), row), pl.BlockSpec((tm, d), row), pl.BlockSpec((e, tm), lambda i: (0, i))),
        out_shape=(jax.ShapeDtypeStruct((n, d), F32), jax.ShapeDtypeStruct((n, d), BF16),
                   jax.ShapeDtypeStruct((e, n), F32)),
        compiler_params=_params(("parallel",)),
        name="out_proj_ln_router",
    )(merged, w_out, x, g, b, w_router_t)


def _route_kernel(lg_ref, rb_ref, su_ref, dest_ref, gw_ref, cnt_ref, counts_ref, run_ref, pstart_ref, *, tb):
    p = pl.program_id(0)
    i = pl.program_id(1)
    ne, tm = lg_ref.shape
    gsz = ne // N_EXPERT_GROUPS
    scores = jax.nn.sigmoid(lg_ref[...])
    biased = scores + rb_ref[...]
    io_in = lax.broadcasted_iota(jnp.int32, (gsz, tm), 0)
    groups = [biased[g * gsz:(g + 1) * gsz, :] for g in range(N_EXPERT_GROUPS)]
    gscore = []
    for rows in groups:
        m1 = rows.max(0, keepdims=True)
        i1 = jnp.where(rows == m1, io_in, gsz).min(0, keepdims=True)
        m2 = jnp.where(io_in == i1, -jnp.inf, rows).max(0, keepdims=True)
        gscore.append(m1 + m2)
    masked = []
    for g in range(N_EXPERT_GROUPS):
        beaten = jnp.zeros((1, tm), F32)
        for o in range(N_EXPERT_GROUPS):
            if o == g:
                continue
            wins = (gscore[o] >= gscore[g]) if o < g else (gscore[o] > gscore[g])
            beaten = beaten + jnp.where(wins, 1.0, 0.0)
        masked.append(jnp.where(beaten < TOPK_GROUPS, groups[g], NEG_INF))
    cur = jnp.concatenate(masked, axis=0)
    io_e = lax.broadcasted_iota(jnp.int32, (ne, tm), 0)
    sel = jnp.zeros((ne, tm), F32)
    for _ in range(TOP_K):
        mx = cur.max(0, keepdims=True)
        first = jnp.where(cur == mx, io_e, ne).min(0, keepdims=True)
        pick = io_e == first
        sel = jnp.where(pick, 1.0, sel)
        cur = jnp.where(pick, -jnp.inf, cur)
    per_expert = sel.sum(1, keepdims=True)

    @pl.when((p == 0) & (i == 0))
    def _():
        counts_ref[...] = jnp.zeros_like(counts_ref)

    @pl.when(p == 0)
    def _():
        counts_ref[...] += per_expert
        dest_ref[...] = jnp.zeros_like(dest_ref)
        gw_ref[...] = jnp.zeros_like(gw_ref)
        cnt_ref[...] = jnp.zeros_like(cnt_ref)

    @pl.when((p == 1) & (i == 0))
    def _():
        padded = jnp.floor((counts_ref[...] + (tb - 1)) / tb) * tb
        r = lax.broadcasted_iota(jnp.int32, (ne, ne), 0)
        c = lax.broadcasted_iota(jnp.int32, (ne, ne), 1)
        padded_l = jnp.where(r == c, padded, 0.0).sum(0, keepdims=True)
        pstart_ref[...] = jnp.where(c < r, padded_l, 0.0).sum(1, keepdims=True)
        run_ref[...] = jnp.zeros_like(run_ref)

    @pl.when(p == 1)
    def _():
        sel_bf = sel.astype(BF16)
        slot = pstart_ref[...] + run_ref[...] + _dot(sel_bf, su_ref[...])
        run_ref[...] += per_expert
        r = lax.broadcasted_iota(jnp.int32, (ne, ne), 0)
        c = lax.broadcasted_iota(jnp.int32, (ne, ne), 1)
        order = _dot(jnp.where(c < r, 1.0, 0.0).astype(BF16), sel_bf)
        picked = sel * scores
        gwm = picked / picked.sum(0, keepdims=True) * ROUTED_SCALE
        for k in range(TOP_K):
            pk = (sel > 0.0) & (order == k)
            dest_ref[k:k + 1, :] = jnp.where(pk, slot, 0.0).sum(0, keepdims=True).astype(jnp.int32)
            gw_ref[k:k + 1, :] = jnp.where(pk, gwm, 0.0).sum(0, keepdims=True)
        cnt_ref[...] = jnp.broadcast_to(counts_ref[...], cnt_ref.shape)


def _route(logits_t, router_bias, tb, tm=512):
    ne, n = logits_t.shape
    strictly_earlier = jnp.asarray(np.triu(np.ones((tm, tm), np.float32), 1), BF16)
    fix = lambda p, i: (0, 0)
    out_blk = lambda p, i: (0, i * p)
    dest, gw, cnt = pl.pallas_call(
        functools.partial(_route_kernel, tb=tb),
        grid=(2, n // tm),
        in_specs=[pl.BlockSpec((ne, tm), lambda p, i: (0, i)), pl.BlockSpec((ne, 1), fix),
                  pl.BlockSpec((tm, tm), fix)],
        out_specs=(pl.BlockSpec((TOP_K, tm), out_blk), pl.BlockSpec((TOP_K, tm), out_blk),
                   pl.BlockSpec((ne, 128), fix)),
        out_shape=(jax.ShapeDtypeStruct((TOP_K, n), jnp.int32), jax.ShapeDtypeStruct((TOP_K, n), F32),
                   jax.ShapeDtypeStruct((ne, 128), F32)),
        scratch_shapes=[pltpu.VMEM((ne, 1), F32)] * 3,
        compiler_params=_params(("arbitrary", "arbitrary")),
        name="router_slots",
    )(logits_t, router_bias.astype(F32)[:, None], strictly_earlier)
    return dest, gw, cnt[:, 0]


def _expert_kernel(be_ref, used_ref, xs_ref, wg_ref, wu_ref, wd_ref, y_ref):
    @pl.when(used_ref[pl.program_id(0)] > 0)
    def _():
        xs = xs_ref[...]
        h = jax.nn.silu(_dot(xs, wg_ref[0])) * _dot(xs, wu_ref[0])
        y_ref[...] = _dot(h.astype(BF16), wd_ref[0]).astype(y_ref.dtype)


def _experts(xs, blk_expert, blk_used, wg, wu, wd, tb):
    ns, d = xs.shape
    hdn = wg.shape[2]
    grid_spec = pltpu.PrefetchScalarGridSpec(
        num_scalar_prefetch=2, grid=(ns // tb,),
        in_specs=[pl.BlockSpec((tb, d), lambda i, be, us: (i, 0)),
                  pl.BlockSpec((1, d, hdn), lambda i, be, us: (be[i], 0, 0)),
                  pl.BlockSpec((1, d, hdn), lambda i, be, us: (be[i], 0, 0)),
                  pl.BlockSpec((1, hdn, d), lambda i, be, us: (be[i], 0, 0))],
        out_specs=pl.BlockSpec((tb, d), lambda i, be, us: (i, 0)))
    return pl.pallas_call(
        _expert_kernel, grid_spec=grid_spec,
        out_shape=jax.ShapeDtypeStruct((ns, d), BF16),
        compiler_params=_params(("arbitrary",)),
        name="grouped_swiglu",
    )(blk_expert, blk_used, xs, wg, wu, wd)


def _final_ln_kernel(x_ref, sh_ref, ys_ref, gw_ref, g_ref, b_ref, xo_ref, xb_ref):
    ffn = sh_ref[...].astype(F32)
    gw = gw_ref[...]
    for kk in range(TOP_K):
        ffn = ffn + gw[:, kk:kk + 1] * ys_ref[kk].astype(F32)
    xn = _layer_norm(ALPHA * x_ref[...] + ffn, g_ref[...], b_ref[...])
    xo_ref[...] = xn
    xb_ref[...] = xn.astype(BF16)


def _final_ln(x, shared, ys_k, gw, g, b, tm=256):
    n, d = x.shape
    row = lambda i: (i, 0)
    fix = lambda i: (0, 0)
    return pl.pallas_call(
        _final_ln_kernel,
        grid=(n // tm,),
        in_specs=[pl.BlockSpec((tm, d), row), pl.BlockSpec((tm, d), row),
                  pl.BlockSpec((TOP_K, tm, d), lambda i: (0, i, 0)), pl.BlockSpec((tm, TOP_K), row),
                  pl.BlockSpec((1, d), fix), pl.BlockSpec((1, d), fix)],
        out_specs=(pl.BlockSpec((tm, d), row), pl.BlockSpec((tm, d), row)),
        out_shape=(jax.ShapeDtypeStruct((n, d), F32), jax.ShapeDtypeStruct((n, d), BF16)),
        compiler_params=_params(("parallel",)),
        name="moe_combine_ln",
    )(x, shared, ys_k, gw, g, b)


def _layer(x, x_bf, biases, w_in, w_o, w_out, sinks, ln1_g, ln1_b, w_router, router_bias,
           w_gate, w_up, w_down, w_sh_gate, w_sh_up, w_sh_down, ln2_g, ln2_b, b, s):
    n, d = x.shape
    bias_a, bias_b, bias_d = biases
    bw = BRANCH_WIDTH
    w_qkv = jnp.pad(w_in[:, :QKV_WIDTH].astype(BF16), ((0, 0), (0, PROJ_WIDTH - QKV_WIDTH)))
    w_gates = w_in[:, QKV_WIDTH:].astype(BF16)
    proj = _matmul(x_bf, w_qkv, BF16, tm=min(1024, n), tn=bw).reshape(b, s, PROJ_WIDTH)

    parts = []
    for dil, ba in zip(A_DILATIONS, bias_a):
        o, lse = _band_attention(proj, QA, KA, VA, bw, ba, dil)
        parts += [o.reshape(n, bw), lse.reshape(n, bw)]
    o_a = _mix_combine(parts)
    o_b = _moba_attention(proj, bias_b).reshape(n, bw)
    o_c = _sb_attention(proj).reshape(n, bw)
    sink_row = jnp.repeat(sinks.astype(F32), HEAD_DIM)[None, :]
    o_d = _band_attention(proj, QD, KD_COL, VD_COL, KV_W, bias_d, 1, sink_row).reshape(n, bw)

    merged = _merge(x_bf, (o_a, o_b, o_c, o_d), w_gates, w_o.astype(BF16))
    x1, x1_bf, logits_t = _out_ln(merged, w_out.astype(BF16), x, ln1_g[None, :], ln1_b[None, :],
                                  w_router.T.astype(BF16))

    tb = EXPERT_BLOCK
    dest, gw, counts = _route(logits_t, router_bias, tb)
    nblk = -(-n * TOP_K // tb) + N_EXPERTS
    pends = jnp.cumsum(jnp.ceil(counts / tb) * tb).astype(jnp.int32)
    blk_start = jnp.arange(nblk, dtype=jnp.int32) * tb
    blk_expert = jnp.minimum((pends[None, :] <= blk_start[:, None]).sum(1), N_EXPERTS - 1).astype(jnp.int32)
    blk_used = (blk_start < pends[-1]).astype(jnp.int32)
    xs = jnp.zeros((nblk * tb, d), BF16)
    for kk in range(TOP_K):
        xs = xs.at[dest[kk]].set(x1_bf, unique_indices=True)
    ys = _experts(xs, blk_expert, blk_used, w_gate.astype(BF16), w_up.astype(BF16), w_down.astype(BF16), tb)
    nsh = n // tb
    shared = _experts(x1_bf, jnp.zeros((nsh,), jnp.int32), jnp.ones((nsh,), jnp.int32),
                      w_sh_gate.astype(BF16)[None], w_sh_up.astype(BF16)[None], w_sh_down.astype(BF16)[None], tb)
    return _final_ln(x1, shared, ys[dest], gw.T, ln2_g[None, :], ln2_b[None, :])


def kernel(x, rel_bias, w_in, w_o, w_out, attn_sinks, ln1_g, ln1_b, w_router, router_bias,
           w_gate, w_up, w_down, w_sh_gate, w_sh_up, w_sh_down, ln2_g, ln2_b):
    b, s, d = x.shape
    n = b * s
    h = MIX_HEADS
    bias_a = [_band_bias(rel_bias[:, :h], win // dil, dil) for win, dil in zip(A_WINDOWS, A_DILATIONS)]
    bias_b = _moba_bias(rel_bias[:, h:2 * h])
    bias_d = _band_bias(rel_bias[:, 2 * h:], SWA_WINDOW - 1, 1)
    xf = x.reshape(n, d)
    xb = xf.astype(BF16)
    for l in range(w_in.shape[0]):
        xf, xb = _layer(xf, xb, (bias_a, bias_b, bias_d), w_in[l], w_o[l], w_out[l], attn_sinks[l],
                        ln1_g[l], ln1_b[l], w_router[l], router_bias[l], w_gate[l], w_up[l], w_down[l],
                        w_sh_gate[l], w_sh_up[l], w_sh_down[l], ln2_g[l], ln2_b[l], b, s)
    return xf.reshape(b, s, d)
```

```python
import functools
import math

import numpy as np
import jax
import jax.numpy as jnp
from jax import lax
from jax.experimental import pallas as pl
from jax.experimental.pallas import tpu as pltpu

D_MODEL = 2048
DEPTH = 4
HEAD_DIM = 64
N_BRANCHES = 4
BRANCH_WIDTH = D_MODEL // N_BRANCHES
MIX_HEADS = BRANCH_WIDTH // HEAD_DIM
SB_HEADS = 4
SB_HEAD_DIM = BRANCH_WIDTH // SB_HEADS
A_WINDOWS = (128, 512, 2048)
A_DILATIONS = (1, 4, 16)
MOBA_BLOCK = 256
MOBA_TOPK = 3
SB_BLOCK = 128
SWA_WINDOW = 128
D_KV_HEADS = 2
BAND_BLOCK = 128
REL_BUCKETS = 32
REL_MAX_EXACT = 16
REL_MAX_DIST = 2048
N_EXPERTS = 64
TOP_K = 8
N_EXPERT_GROUPS = 8
TOPK_GROUPS = 4
EXPERT_HIDDEN = D_MODEL * 3 // 16
ROUTED_SCALE = 2.5
EXPERT_BLOCK = 512
LN_EPS = 1e-5
ALPHA = (2 * DEPTH) ** 0.25
NEG_INF = -1e30
KV_W = D_KV_HEADS * HEAD_DIM
QKV_WIDTH = 10 * BRANCH_WIDTH + 2 * KV_W
MOBA_BIAS_TILES = 8
QA, KA, VA = range(3)
PROJ_A_WIDTH = 3 * BRANCH_WIDTH
QB, KB, VB, QC, KC, VC, QD = range(7)
PROJ_R_WIDTH = 8 * BRANCH_WIDTH
KD_COL = 7 * BRANCH_WIDTH // KV_W
VD_COL = KD_COL + 1
F32_EXP_ZERO_BELOW = -104.0

VMEM_LIMIT_BYTES = 56 * 1024 * 1024

F32 = jnp.float32
BF16 = jnp.bfloat16


def _params(semantics, vmem=VMEM_LIMIT_BYTES):
    return pltpu.CompilerParams(dimension_semantics=semantics, vmem_limit_bytes=vmem)


def _dot(a, b):
    return jnp.dot(a, b, preferred_element_type=F32)


def _dot_nt(a, b):
    return lax.dot_general(a, b, (((1,), (1,)), ((), ())), preferred_element_type=F32)


def _mm_kernel(x_ref, w_ref, o_ref):
    o_ref[...] = _dot(x_ref[...], w_ref[...]).astype(o_ref.dtype)


def _matmul(x, w, out_dtype, tm, tn):
    m, k = x.shape
    n = w.shape[1]
    return pl.pallas_call(
        _mm_kernel,
        grid=(m // tm, n // tn),
        in_specs=[pl.BlockSpec((tm, k), lambda i, j: (i, 0)),
                  pl.BlockSpec((k, tn), lambda i, j: (0, j))],
        out_specs=pl.BlockSpec((tm, tn), lambda i, j: (i, j)),
        out_shape=jax.ShapeDtypeStruct((m, n), out_dtype),
        compiler_params=_params(("parallel", "arbitrary")),
        name="qkv_proj",
    )(x, w)


def _t5_bucket(dist):
    n = jnp.maximum(dist, 0)
    nf = jnp.maximum(n, 1).astype(F32)
    large = REL_MAX_EXACT + (jnp.log(nf / REL_MAX_EXACT) / math.log(REL_MAX_DIST / REL_MAX_EXACT)
                             * (REL_BUCKETS - REL_MAX_EXACT)).astype(jnp.int32)
    return jnp.where(n < REL_MAX_EXACT, n, jnp.minimum(large, REL_BUCKETS - 1))


def _band_bias(tab, max_steps, dist_scale):
    qi = np.arange(BAND_BLOCK)[:, None]
    kj = np.arange(2 * BAND_BLOCK)[None, :]
    dist = BAND_BLOCK + qi - kj
    band = (dist >= 0) & (dist <= max_steps)
    bias = tab[_t5_bucket(jnp.asarray(dist * dist_scale, jnp.int32))].astype(F32)
    bias = jnp.where(jnp.asarray(band)[:, :, None], bias, NEG_INF)
    return bias.transpose(2, 0, 1)


def _moba_bias(tab):
    r = np.arange(MOBA_BLOCK)[:, None]
    c = np.arange(MOBA_BLOCK)[None, :]
    dist = np.stack([dl * MOBA_BLOCK + r - c for dl in range(MOBA_BIAS_TILES)])
    bias = tab[_t5_bucket(jnp.asarray(dist, jnp.int32))].astype(F32)
    bias = jnp.where(jnp.asarray(dist >= 0)[..., None], bias, NEG_INF)
    return bias.transpose(3, 0, 1, 2)


def _expand_kv(t, kv_rep):
    assert t.shape[1] == 2 * HEAD_DIM and kv_rep * 2 == MIX_HEADS
    tf = t.astype(F32)
    swapped = pltpu.roll(tf, HEAD_DIM, axis=1)
    first = lax.broadcasted_iota(jnp.int32, (1, 2 * HEAD_DIM), 1) < HEAD_DIM
    g0 = jnp.where(first, tf, swapped).astype(t.dtype)
    g1 = jnp.where(first, swapped, tf).astype(t.dtype)
    return jnp.concatenate([g0] * (kv_rep // 2) + [g1] * (kv_rep // 2), axis=1)


def _band_kernel(*refs, kv_rep, with_sink):
    if with_sink:
        q_ref, kp_ref, ko_ref, vp_ref, vo_ref, bias_ref, sink_ref, o_ref = refs
    else:
        q_ref, kp_ref, ko_ref, vp_ref, vo_ref, bias_ref, o_ref, lse_ref = refs
    has_prev = pl.program_id(2) > 0
    blk = BAND_BLOCK
    q = q_ref[0]
    lane = lax.broadcasted_iota(jnp.int32, (1, q.shape[1]), 1)
    in_head = [(lane >= h * HEAD_DIM) & (lane < (h + 1) * HEAD_DIM) for h in range(MIX_HEADS)]
    zero = jnp.zeros_like(q)
    q_stack = jnp.concatenate([jnp.where(in_head[h], q, zero) for h in range(MIX_HEADS)], axis=0)
    q_stack = q_stack * (HEAD_DIM ** -0.5)
    k = jnp.concatenate([kp_ref[0], ko_ref[0]], axis=0)
    v = jnp.concatenate([vp_ref[0], vo_ref[0]], axis=0)
    if kv_rep > 1:
        k, v = _expand_kv(k, kv_rep), _expand_kv(v, kv_rep)
    s = _dot_nt(q_stack, k) + bias_ref[...].reshape(MIX_HEADS * blk, 2 * blk)
    own_cols = lax.broadcasted_iota(jnp.int32, (1, 2 * blk), 1) >= blk
    s = jnp.where(jnp.logical_or(own_cols, has_prev), s, NEG_INF)
    m = s.max(-1, keepdims=True)
    p = jnp.exp(s - m)
    den = p.sum(-1, keepdims=True)
    o_all = _dot(p.astype(BF16), v) / den
    lse_all = m + jnp.log(den)
    o = o_all[:blk]
    lse = jnp.broadcast_to(lse_all[:blk], o.shape)
    for h in range(1, MIX_HEADS):
        rows = slice(h * blk, (h + 1) * blk)
        o = jnp.where(in_head[h], o_all[rows], o)
        lse = jnp.where(in_head[h], lse_all[rows], lse)
    if with_sink:
        o_ref[0] = (o * jax.nn.sigmoid(lse - sink_ref[...])).astype(o_ref.dtype)
    else:
        o_ref[0] = o.astype(o_ref.dtype)
        lse_ref[0] = lse


def _band_attention(proj, qcol, kcol, vcol, kvw, bias, dil, sink_row=None):
    b, s, pw = proj.shape
    qw = BRANCH_WIDTH
    l = s // dil
    nb = l // BAND_BLOCK
    kv_rep = qw // kvw
    pv = proj.reshape(b, l, dil * pw)
    qpr, kpr = pw // qw, pw // kvw
    own = lambda bi, r, n: (bi, n, r)
    q_own = lambda bi, r, n: (bi, n, r * qpr + qcol)
    k_own = lambda bi, r, n: (bi, n, r * kpr + kcol)
    k_prev = lambda bi, r, n: (bi, jnp.maximum(n - 1, 0), r * kpr + kcol)
    v_own = lambda bi, r, n: (bi, n, r * kpr + vcol)
    v_prev = lambda bi, r, n: (bi, jnp.maximum(n - 1, 0), r * kpr + vcol)
    in_specs = [pl.BlockSpec((1, BAND_BLOCK, qw), q_own),
                pl.BlockSpec((1, BAND_BLOCK, kvw), k_prev),
                pl.BlockSpec((1, BAND_BLOCK, kvw), k_own),
                pl.BlockSpec((1, BAND_BLOCK, kvw), v_prev),
                pl.BlockSpec((1, BAND_BLOCK, kvw), v_own),
                pl.BlockSpec((MIX_HEADS, BAND_BLOCK, 2 * BAND_BLOCK), lambda bi, r, n: (0, 0, 0))]
    args = [pv, pv, pv, pv, pv, bias]
    with_sink = sink_row is not None
    if with_sink:
        in_specs.append(pl.BlockSpec((1, qw), lambda bi, r, n: (0, 0)))
        args.append(sink_row)
        out_shape = jax.ShapeDtypeStruct((b, l, dil * qw), BF16)
        out_specs = pl.BlockSpec((1, BAND_BLOCK, qw), own)
    else:
        out_shape = (jax.ShapeDtypeStruct((b, l, dil * qw), F32),) * 2
        out_specs = (pl.BlockSpec((1, BAND_BLOCK, qw), own),) * 2
    out = pl.pallas_call(
        functools.partial(_band_kernel, kv_rep=kv_rep, with_sink=with_sink),
        grid=(b, dil, nb),
        in_specs=in_specs, out_specs=out_specs, out_shape=out_shape,
        compiler_params=_params(("parallel", "parallel", "arbitrary")),
        name="band_attn_sink" if with_sink else f"band_attn_d{dil}",
    )(*args)
    if with_sink:
        return out.reshape(b, s, qw)
    return out[0].reshape(b, s, qw), out[1].reshape(b, s, qw)


def _mix_combine_kernel(o1, l1, o2, l2, o3, l3, out_ref):
    a, bq, c = l1[...], l2[...], l3[...]
    m = jnp.maximum(jnp.maximum(a, bq), c)
    ea, eb, ec = jnp.exp(a - m), jnp.exp(bq - m), jnp.exp(c - m)
    den = ea + eb + ec
    out_ref[...] = (ea / den * o1[...] + eb / den * o2[...] + ec / den * o3[...]).astype(out_ref.dtype)


def _mix_combine(parts, tm=512):
    n, w = parts[0].shape
    spec = pl.BlockSpec((tm, w), lambda i: (i, 0))
    return pl.pallas_call(
        _mix_combine_kernel, grid=(n // tm,),
        in_specs=[spec] * 6, out_specs=spec,
        out_shape=jax.ShapeDtypeStruct((n, w), BF16),
        compiler_params=_params(("parallel",)),
        name="dilated_combine",
    )(*parts)


def _moba_kernel(q_ref, k_ref, v_ref, bias_ref, o_ref, kmean_ref, *, nbk):
    i = pl.program_id(2)
    blk = MOBA_BLOCK
    pair_w = 2 * HEAD_DIM

    nslot = kmean_ref.shape[0]
    assert nbk <= nslot

    @pl.when(i == 0)
    def _():
        kk = k_ref[0].astype(F32).reshape(nbk, blk, pair_w)
        kmean_ref[...] = jnp.zeros_like(kmean_ref)
        kmean_ref[0:nbk, :] = kk.sum(axis=1) * (1.0 / blk)

    q = q_ref[0]
    lane = lax.broadcasted_iota(jnp.int32, (1, pair_w), 1)
    zero = jnp.zeros_like(q)
    qs = jnp.concatenate([jnp.where(lane < HEAD_DIM, q, zero), jnp.where(lane >= HEAD_DIM, q, zero)], axis=0)
    qsc = qs * (HEAD_DIM ** -0.5)
    col = lax.broadcasted_iota(jnp.int32, (2 * blk, nslot), 1)
    gate = jnp.where(col < i, _dot_nt(qs, kmean_ref[...].astype(BF16)), NEG_INF)
    sel = jnp.zeros((2 * blk, nslot), F32)
    for _ in range(MOBA_TOPK):
        mx = gate.max(-1, keepdims=True)
        first = jnp.where(gate == mx, col, nslot).min(-1, keepdims=True)
        pick = col == first
        sel = jnp.where(pick & (col < i), 1.0, sel)
        gate = jnp.where(pick, -jnp.inf, gate)

    def bias_tile(dl):
        dl = jnp.clip(dl, 0, MOBA_BIAS_TILES - 1)
        return jnp.concatenate([bias_ref[0, dl], bias_ref[1, dl]], axis=0)

    own = pl.ds(pl.multiple_of(i * blk, blk), blk)
    s = _dot_nt(qsc, k_ref[0, own, :]) + bias_tile(0)
    m0 = s.max(-1, keepdims=True)
    p = jnp.exp(s - m0)
    l0 = p.sum(-1, keepdims=True)
    acc0 = _dot(p.astype(BF16), v_ref[0, own, :])
    q_aug = jnp.concatenate([qsc, jnp.where(sel > 0.0, 0.0, NEG_INF).astype(BF16)], axis=1)
    tag_of = (lax.broadcasted_iota(jnp.int32, (2 * blk, nslot), 1)
              - lax.broadcasted_iota(jnp.int32, (2 * blk, nslot), 0) // blk)

    def body(c, carry):
        m, l, acc = carry
        ja = 2 * c
        rows = pl.ds(pl.multiple_of(ja * blk, 2 * blk), 2 * blk)
        key_tag = jnp.where(tag_of == ja, 1.0, 0.0).astype(BF16)
        bias = jnp.concatenate([bias_tile(i - ja), bias_tile(i - ja - 1)], axis=1)
        s = _dot_nt(q_aug, jnp.concatenate([k_ref[0, rows, :], key_tag], axis=1)) + bias
        m_new = jnp.maximum(m, s.max(-1, keepdims=True))
        scale = jnp.exp(m - m_new)
        p = jnp.exp(s - m_new)
        l = scale * l + p.sum(-1, keepdims=True)
        acc = scale * acc + _dot(p.astype(BF16), v_ref[0, rows, :])
        return m_new, l, acc

    _, l, acc = lax.fori_loop(0, (i + 1) // 2, body, (m0, l0, acc0))
    o = acc / l
    o_ref[0] = jnp.where(lane < HEAD_DIM, o[:blk], o[blk:]).astype(o_ref.dtype)


def _moba_attention(proj, bias):
    b, s, pw = proj.shape
    nbk = s // MOBA_BLOCK
    pair_w = 2 * HEAD_DIM
    npair = BRANCH_WIDTH // pair_w
    slab = BRANCH_WIDTH // pair_w
    return pl.pallas_call(
        functools.partial(_moba_kernel, nbk=nbk),
        grid=(b, npair, nbk),
        in_specs=[pl.BlockSpec((1, MOBA_BLOCK, pair_w), lambda bi, p, i: (bi, i, QB * slab + p)),
                  pl.BlockSpec((1, s, pair_w), lambda bi, p, i: (bi, 0, KB * slab + p)),
                  pl.BlockSpec((1, s, pair_w), lambda bi, p, i: (bi, 0, VB * slab + p)),
                  pl.BlockSpec((2, MOBA_BIAS_TILES, MOBA_BLOCK, MOBA_BLOCK), lambda bi, p, i: (p, 0, 0, 0))],
        out_specs=pl.BlockSpec((1, MOBA_BLOCK, pair_w), lambda bi, p, i: (bi, i, p)),
        out_shape=jax.ShapeDtypeStruct((b, s, BRANCH_WIDTH), BF16),
        scratch_shapes=[pltpu.VMEM((pair_w, pair_w), F32)],
        compiler_params=_params(("parallel", "parallel", "arbitrary")),
        name="moba_attn",
    )(proj, proj, proj, bias)


def _sb_kernel(q_ref, k_ref, v_ref, o_ref, kmax_ref):
    i = pl.program_id(1)
    blk = SB_BLOCK
    hd = SB_HEAD_DIM
    s_len = k_ref.shape[1]
    scale = hd ** -0.5

    @pl.when(i == 0)
    def _():
        chunk = 512
        for h in range(SB_HEADS):
            def norm_max(c, mx):
                kk = k_ref[0, pl.ds(pl.multiple_of(c * chunk, chunk), chunk), h * hd:(h + 1) * hd].astype(F32)
                return jnp.maximum(mx, (kk * kk).sum(-1, keepdims=True).max(0, keepdims=True))
            mx = lax.fori_loop(0, s_len // chunk, norm_max, jnp.zeros((1, 1), F32))
            kmax_ref[h] = jnp.broadcast_to(jnp.sqrt(mx), kmax_ref.shape[1:])

    row = lax.broadcasted_iota(jnp.int32, (blk, blk), 0)
    colb = lax.broadcasted_iota(jnp.int32, (blk, blk), 1)
    strictly_past = colb < row
    r2 = lax.broadcasted_iota(jnp.int32, (blk, 2 * blk), 0)
    c2 = lax.broadcasted_iota(jnp.int32, (blk, 2 * blk), 1)
    tri_ones = jnp.where((c2 >= blk) | (r2 >= c2), 1.0, 0.0).astype(BF16)
    qs = [q_ref[0, :, h * hd:(h + 1) * hd] for h in range(SB_HEADS)]
    z_bound = []
    for h in range(SB_HEADS):
        qf = qs[h].astype(F32)
        z_bound.append(jnp.sqrt((qf * qf).sum(-1, keepdims=True)) * kmax_ref[h][0:1, 0:1] * scale)

    def block(h, j, carry, acc, diagonal):
        rows = pl.ds(pl.multiple_of(j * blk, blk), blk)
        kj = k_ref[0, rows, h * hd:(h + 1) * hd]
        vj = v_ref[0, rows, h * hd:(h + 1) * hd]
        z = _dot_nt(qs[h], kj) * scale
        lk = -(jnp.maximum(z, 0.0) + jnp.log1p(jnp.exp(-jnp.abs(z))))
        if diagonal:
            lk = jnp.where(strictly_past, lk, 0.0)
        sums = _dot(lk.astype(BF16), tri_ones)
        log_a = sums[:, :blk] + carry + z
        if diagonal:
            log_a = jnp.where(strictly_past, log_a, NEG_INF)
        acc = acc + _dot(jnp.exp(log_a).astype(BF16), vj)
        return carry + sums[:, blk:], acc

    def all_dead(carries):
        worst = None
        for h in range(SB_HEADS):
            w = (carries[h] * 0.98 + z_bound[h] * 1.02).max()
            worst = w if worst is None else jnp.maximum(worst, w)
        return worst < F32_EXP_ZERO_BELOW - 6.0

    state = [block(h, i, jnp.zeros((blk, blk), F32), jnp.zeros((blk, hd), F32), True) for h in range(SB_HEADS)]
    carries = tuple(st[0] for st in state)
    accs = tuple(st[1] for st in state)

    def cond(c):
        t, dead, _, _ = c
        return (t < i) & jnp.logical_not(dead)

    def body(c):
        t, _, carries, accs = c
        new = [block(h, i - 1 - t, carries[h], accs[h], False) for h in range(SB_HEADS)]
        carries = tuple(st[0] for st in new)
        return t + 1, all_dead(carries), carries, tuple(st[1] for st in new)

    _, _, _, accs = lax.while_loop(cond, body, (jnp.int32(0), all_dead(carries), carries, accs))
    for h in range(SB_HEADS):
        o_ref[0, :, h * hd:(h + 1) * hd] = accs[h].astype(o_ref.dtype)


def _sb_attention(proj):
    b, s, pw = proj.shape
    nq = s // SB_BLOCK
    w = BRANCH_WIDTH
    return pl.pallas_call(
        _sb_kernel,
        grid=(b, nq),
        in_specs=[pl.BlockSpec((1, SB_BLOCK, w), lambda bi, i: (bi, i, QC)),
                  pl.BlockSpec((1, s, w), lambda bi, i: (bi, 0, KC)),
                  pl.BlockSpec((1, s, w), lambda bi, i: (bi, 0, VC))],
        out_specs=pl.BlockSpec((1, SB_BLOCK, w), lambda bi, i: (bi, i, 0)),
        out_shape=jax.ShapeDtypeStruct((b, s, w), BF16),
        scratch_shapes=[pltpu.VMEM((SB_HEADS, 8, 128), F32)],
        compiler_params=_params(("parallel", "arbitrary")),
        name="stick_breaking_attn",
    )(proj, proj, proj)


def _merge_kernel(x_ref, oa_ref, ob_ref, oc_ref, od_ref, g0, g1, g2, g3, w0, w1, w2, w3, out_ref):
    x = x_ref[...]
    acc = None
    for o_ref, wg_ref, wo_ref in ((oa_ref, g0, w0), (ob_ref, g1, w1), (oc_ref, g2, w2), (od_ref, g3, w3)):
        gate = jax.nn.sigmoid(_dot(x, wg_ref[...]))
        term = gate * _dot(o_ref[...], wo_ref[...])
        acc = term if acc is None else acc + term
    out_ref[...] = acc.astype(out_ref.dtype)


def _merge(x_bf, branches, w_gates, w_o, tm=512, tn=512):
    n, d = x_bf.shape
    nj = d // tn
    row = lambda j, i: (i, 0)
    in_specs = [pl.BlockSpec((tm, d), row)] + [pl.BlockSpec((tm, BRANCH_WIDTH), row)] * N_BRANCHES
    in_specs += [pl.BlockSpec((d, tn), functools.partial(lambda j, i, br: (0, br * nj + j), br=br))
                 for br in range(N_BRANCHES)]
    in_specs += [pl.BlockSpec((BRANCH_WIDTH, tn), functools.partial(lambda j, i, br: (br, j), br=br))
                 for br in range(N_BRANCHES)]
    return pl.pallas_call(
        _merge_kernel,
        grid=(nj, n // tm),
        in_specs=in_specs,
        out_specs=pl.BlockSpec((tm, tn), lambda j, i: (i, j)),
        out_shape=jax.ShapeDtypeStruct((n, d), BF16),
        compiler_params=_params(("parallel", "arbitrary")),
        name="gated_merge",
    )(x_bf, *branches, *([w_gates] * N_BRANCHES), *([w_o] * N_BRANCHES))


def _layer_norm(r, g, b):
    mu = r.mean(-1, keepdims=True)
    var = jnp.square(r - mu).mean(-1, keepdims=True)
    return (r - mu) * lax.rsqrt(var + LN_EPS) * g + b


def _out_ln_kernel(m_ref, w_ref, x_ref, g_ref, b_ref, wrt_ref, xo_ref, xb_ref, lgt_ref):
    y = _dot(m_ref[...], w_ref[...])
    xn = _layer_norm(ALPHA * x_ref[...] + y, g_ref[...], b_ref[...])
    xo_ref[...] = xn
    xb = xn.astype(BF16)
    xb_ref[...] = xb
    lgt_ref[...] = _dot_nt(wrt_ref[...], xb)


def _out_ln(merged, w_out, x, g, b, w_router_t, tm=512):
    n, d = x.shape
    e = w_router_t.shape[0]
    row = lambda i: (i, 0)
    fix = lambda i: (0, 0)
    return pl.pallas_call(
        _out_ln_kernel,
        grid=(n // tm,),
        in_specs=[pl.BlockSpec((tm, d), row), pl.BlockSpec((d, d), fix), pl.BlockSpec((tm, d), row),
                  pl.BlockSpec((1, d), fix), pl.BlockSpec((1, d), fix), pl.BlockSpec((e, d), fix)],
        out_specs=(pl.BlockSpec((tm, d), row), pl.BlockSpec((tm, d), row), pl.BlockSpec((e, tm), lambda i: (0, i))),
        out_shape=(jax.ShapeDtypeStruct((n, d), F32), jax.ShapeDtypeStruct((n, d), BF16),
                   jax.ShapeDtypeStruct((e, n), F32)),
        compiler_params=_params(("parallel",)),
        name="out_proj_ln_router",
    )(merged, w_out, x, g, b, w_router_t)


def _route_kernel(lg_ref, rb_ref, su_ref, dest_ref, gw_ref, cnt_ref, counts_ref, run_ref, pstart_ref, *, tb):
    p = pl.program_id(0)
    i = pl.program_id(1)
    ne, tm = lg_ref.shape
    gsz = ne // N_EXPERT_GROUPS
    scores = jax.nn.sigmoid(lg_ref[...])
    biased = scores + rb_ref[...]
    io_in = lax.broadcasted_iota(jnp.int32, (gsz, tm), 0)
    groups = [biased[g * gsz:(g + 1) * gsz, :] for g in range(N_EXPERT_GROUPS)]
    gscore = []
    for rows in groups:
        m1 = rows.max(0, keepdims=True)
        i1 = jnp.where(rows == m1, io_in, gsz).min(0, keepdims=True)
        m2 = jnp.where(io_in == i1, -jnp.inf, rows).max(0, keepdims=True)
        gscore.append(m1 + m2)
    masked = []
    for g in range(N_EXPERT_GROUPS):
        beaten = jnp.zeros((1, tm), F32)
        for o in range(N_EXPERT_GROUPS):
            if o == g:
                continue
            wins = (gscore[o] >= gscore[g]) if o < g else (gscore[o] > gscore[g])
            beaten = beaten + jnp.where(wins, 1.0, 0.0)
        masked.append(jnp.where(beaten < TOPK_GROUPS, groups[g], NEG_INF))
    cur = jnp.concatenate(masked, axis=0)
    io_e = lax.broadcasted_iota(jnp.int32, (ne, tm), 0)
    sel = jnp.zeros((ne, tm), F32)
    for _ in range(TOP_K):
        mx = cur.max(0, keepdims=True)
        first = jnp.where(cur == mx, io_e, ne).min(0, keepdims=True)
        pick = io_e == first
        sel = jnp.where(pick, 1.0, sel)
        cur = jnp.where(pick, -jnp.inf, cur)
    per_expert = sel.sum(1, keepdims=True)

    @pl.when((p == 0) & (i == 0))
    def _():
        counts_ref[...] = jnp.zeros_like(counts_ref)

    @pl.when(p == 0)
    def _():
        counts_ref[...] += per_expert
        dest_ref[...] = jnp.zeros_like(dest_ref)
        gw_ref[...] = jnp.zeros_like(gw_ref)
        cnt_ref[...] = jnp.zeros_like(cnt_ref)

    @pl.when((p == 1) & (i == 0))
    def _():
        padded = jnp.floor((counts_ref[...] + (tb - 1)) / tb) * tb
        r = lax.broadcasted_iota(jnp.int32, (ne, ne), 0)
        c = lax.broadcasted_iota(jnp.int32, (ne, ne), 1)
        padded_l = jnp.where(r == c, padded, 0.0).sum(0, keepdims=True)
        pstart_ref[...] = jnp.where(c < r, padded_l, 0.0).sum(1, keepdims=True)
        run_ref[...] = jnp.zeros_like(run_ref)

    @pl.when(p == 1)
    def _():
        sel_bf = sel.astype(BF16)
        slot = pstart_ref[...] + run_ref[...] + _dot(sel_bf, su_ref[...])
        run_ref[...] += per_expert
        r = lax.broadcasted_iota(jnp.int32, (ne, ne), 0)
        c = lax.broadcasted_iota(jnp.int32, (ne, ne), 1)
        order = _dot(jnp.where(c < r, 1.0, 0.0).astype(BF16), sel_bf)
        picked = sel * scores
        gwm = picked / picked.sum(0, keepdims=True) * ROUTED_SCALE
        for k in range(TOP_K):
            pk = (sel > 0.0) & (order == k)
            dest_ref[k:k + 1, :] = jnp.where(pk, slot, 0.0).sum(0, keepdims=True).astype(jnp.int32)
            gw_ref[k:k + 1, :] = jnp.where(pk, gwm, 0.0).sum(0, keepdims=True)
        cnt_ref[...] = jnp.broadcast_to(counts_ref[...], cnt_ref.shape)


def _route(logits_t, router_bias, tb, tm=512):
    ne, n = logits_t.shape
    strictly_earlier = jnp.asarray(np.triu(np.ones((tm, tm), np.float32), 1), BF16)
    fix = lambda p, i: (0, 0)
    out_blk = lambda p, i: (0, i * p)
    dest, gw, cnt = pl.pallas_call(
        functools.partial(_route_kernel, tb=tb),
        grid=(2, n // tm),
        in_specs=[pl.BlockSpec((ne, tm), lambda p, i: (0, i)), pl.BlockSpec((ne, 1), fix),
                  pl.BlockSpec((tm, tm), fix)],
        out_specs=(pl.BlockSpec((TOP_K, tm), out_blk), pl.BlockSpec((TOP_K, tm), out_blk),
                   pl.BlockSpec((ne, 128), fix)),
        out_shape=(jax.ShapeDtypeStruct((TOP_K, n), jnp.int32), jax.ShapeDtypeStruct((TOP_K, n), F32),
                   jax.ShapeDtypeStruct((ne, 128), F32)),
        scratch_shapes=[pltpu.VMEM((ne, 1), F32)] * 3,
        compiler_params=_params(("arbitrary", "arbitrary")),
        name="router_slots",
    )(logits_t, router_bias.astype(F32)[:, None], strictly_earlier)
    return dest, gw, cnt[:, 0]


def _expert_kernel(be_ref, used_ref, xs_ref, wg_ref, wu_ref, wd_ref, y_ref):
    @pl.when(used_ref[pl.program_id(0)] > 0)
    def _():
        xs = xs_ref[...]
        h = jax.nn.silu(_dot(xs, wg_ref[0])) * _dot(xs, wu_ref[0])
        y_ref[...] = _dot(h.astype(BF16), wd_ref[0]).astype(y_ref.dtype)


def _experts(xs, blk_expert, blk_used, wg, wu, wd, tb):
    ns, d = xs.shape
    hdn = wg.shape[2]
    grid_spec = pltpu.PrefetchScalarGridSpec(
        num_scalar_prefetch=2, grid=(ns // tb,),
        in_specs=[pl.BlockSpec((tb, d), lambda i, be, us: (i, 0)),
                  pl.BlockSpec((1, d, hdn), lambda i, be, us: (be[i], 0, 0)),
                  pl.BlockSpec((1, d, hdn), lambda i, be, us: (be[i], 0, 0)),
                  pl.BlockSpec((1, hdn, d), lambda i, be, us: (be[i], 0, 0))],
        out_specs=pl.BlockSpec((tb, d), lambda i, be, us: (i, 0)))
    return pl.pallas_call(
        _expert_kernel, grid_spec=grid_spec,
        out_shape=jax.ShapeDtypeStruct((ns, d), BF16),
        compiler_params=_params(("arbitrary",)),
        name="grouped_swiglu",
    )(blk_expert, blk_used, xs, wg, wu, wd)


def _final_ln_kernel(x_ref, sh_ref, ys_ref, gw_ref, g_ref, b_ref, xo_ref, xb_ref):
    ffn = sh_ref[...].astype(F32)
    gw = gw_ref[...]
    for kk in range(TOP_K):
        ffn = ffn + gw[:, kk:kk + 1] * ys_ref[kk].astype(F32)
    xn = _layer_norm(ALPHA * x_ref[...] + ffn, g_ref[...], b_ref[...])
    xo_ref[...] = xn
    xb_ref[...] = xn.astype(BF16)


def _final_ln(x, shared, ys_k, gw, g, b, tm=256):
    n, d = x.shape
    row = lambda i: (i, 0)
    fix = lambda i: (0, 0)
    return pl.pallas_call(
        _final_ln_kernel,
        grid=(n // tm,),
        in_specs=[pl.BlockSpec((tm, d), row), pl.BlockSpec((tm, d), row),
                  pl.BlockSpec((TOP_K, tm, d), lambda i: (0, i, 0)), pl.BlockSpec((tm, TOP_K), row),
                  pl.BlockSpec((1, d), fix), pl.BlockSpec((1, d), fix)],
        out_specs=(pl.BlockSpec((tm, d), row), pl.BlockSpec((tm, d), row)),
        out_shape=(jax.ShapeDtypeStruct((n, d), F32), jax.ShapeDtypeStruct((n, d), BF16)),
        compiler_params=_params(("parallel",)),
        name="moe_combine_ln",
    )(x, shared, ys_k, gw, g, b)


def _layer(x, x_bf, biases, w_in, w_o, w_out, sinks, ln1_g, ln1_b, w_router, router_bias,
           w_gate, w_up, w_down, w_sh_gate, w_sh_up, w_sh_down, ln2_g, ln2_b, b, s):
    n, d = x.shape
    bias_a, bias_b, bias_d = biases
    bw = BRANCH_WIDTH
    w_a = w_in[:, :PROJ_A_WIDTH].astype(BF16)
    w_r = jnp.pad(w_in[:, PROJ_A_WIDTH:QKV_WIDTH].astype(BF16),
                  ((0, 0), (0, PROJ_R_WIDTH - (QKV_WIDTH - PROJ_A_WIDTH))))
    w_gates = w_in[:, QKV_WIDTH:].astype(BF16)
    proj_a = _matmul(x_bf, w_a, BF16, tm=min(1024, n), tn=bw).reshape(b, s, PROJ_A_WIDTH)
    proj = _matmul(x_bf, w_r, BF16, tm=min(1024, n), tn=bw).reshape(b, s, PROJ_R_WIDTH)

    parts = []
    for dil, ba in zip(A_DILATIONS, bias_a):
        o, lse = _band_attention(proj_a, QA, KA, VA, bw, ba, dil)
        parts += [o.reshape(n, bw), lse.reshape(n, bw)]
    o_a = _mix_combine(parts)
    o_b = _moba_attention(proj, bias_b).reshape(n, bw)
    o_c = _sb_attention(proj).reshape(n, bw)
    sink_row = jnp.repeat(sinks.astype(F32), HEAD_DIM)[None, :]
    o_d = _band_attention(proj, QD, KD_COL, VD_COL, KV_W, bias_d, 1, sink_row).reshape(n, bw)

    merged = _merge(x_bf, (o_a, o_b, o_c, o_d), w_gates, w_o.astype(BF16))
    x1, x1_bf, logits_t = _out_ln(merged, w_out.astype(BF16), x, ln1_g[None, :], ln1_b[None, :],
                                  w_router.T.astype(BF16))

    tb = EXPERT_BLOCK
    dest, gw, counts = _route(logits_t, router_bias, tb)
    nblk = -(-n * TOP_K // tb) + N_EXPERTS
    pends = jnp.cumsum(jnp.ceil(counts / tb) * tb).astype(jnp.int32)
    blk_start = jnp.arange(nblk, dtype=jnp.int32) * tb
    blk_expert = jnp.minimum((pends[None, :] <= blk_start[:, None]).sum(1), N_EXPERTS - 1).astype(jnp.int32)
    blk_used = (blk_start < pends[-1]).astype(jnp.int32)
    token = jnp.broadcast_to(jnp.arange(n, dtype=jnp.int32)[None, :], dest.shape)
    slot_token = jnp.zeros((nblk * tb,), jnp.int32).at[dest.reshape(-1)].set(token.reshape(-1), unique_indices=True)
    xs = x1_bf[slot_token]
    ys = _experts(xs, blk_expert, blk_used, w_gate.astype(BF16), w_up.astype(BF16), w_down.astype(BF16), tb)
    nsh = n // tb
    shared = _experts(x1_bf, jnp.zeros((nsh,), jnp.int32), jnp.ones((nsh,), jnp.int32),
                      w_sh_gate.astype(BF16)[None], w_sh_up.astype(BF16)[None], w_sh_down.astype(BF16)[None], tb)
    return _final_ln(x1, shared, ys[dest], gw.T, ln2_g[None, :], ln2_b[None, :])


def kernel(x, rel_bias, w_in, w_o, w_out, attn_sinks, ln1_g, ln1_b, w_router, router_bias,
           w_gate, w_up, w_down, w_sh_gate, w_sh_up, w_sh_down, ln2_g, ln2_b):
    b, s, d = x.shape
    n = b * s
    h = MIX_HEADS
    bias_a = [_band_bias(rel_bias[:, :h], win // dil, dil) for win, dil in zip(A_WINDOWS, A_DILATIONS)]
    bias_b = _moba_bias(rel_bias[:, h:2 * h])
    bias_d = _band_bias(rel_bias[:, 2 * h:], SWA_WINDOW - 1, 1)
    xf = x.reshape(n, d)
    xb = xf.astype(BF16)
    for l in range(w_in.shape[0]):
        xf, xb = _layer(xf, xb, (bias_a, bias_b, bias_d), w_in[l], w_o[l], w_out[l], attn_sinks[l],
                        ln1_g[l], ln1_b[l], w_router[l], router_bias[l], w_gate[l], w_up[l], w_down[l],
                        w_sh_gate[l], w_sh_up[l], w_sh_down[l], ln2_g[l], ln2_b[l], b, s)
    return xf.reshape(b, s, d)
```

```python
import functools
import math

import numpy as np
import jax
import jax.numpy as jnp
from jax import lax
from jax.experimental import pallas as pl
from jax.experimental.pallas import tpu as pltpu

D_MODEL = 2048
DEPTH = 4
HEAD_DIM = 64
N_BRANCHES = 4
BRANCH_WIDTH = D_MODEL // N_BRANCHES
MIX_HEADS = BRANCH_WIDTH // HEAD_DIM
SB_HEADS = 4
SB_HEAD_DIM = BRANCH_WIDTH // SB_HEADS
A_WINDOWS = (128, 512, 2048)
A_DILATIONS = (1, 4, 16)
MOBA_BLOCK = 256
MOBA_TOPK = 3
SB_BLOCK = 128
SWA_WINDOW = 128
D_KV_HEADS = 2
BAND_BLOCK = 128
REL_BUCKETS = 32
REL_MAX_EXACT = 16
REL_MAX_DIST = 2048
N_EXPERTS = 64
TOP_K = 8
N_EXPERT_GROUPS = 8
TOPK_GROUPS = 4
EXPERT_HIDDEN = D_MODEL * 3 // 16
ROUTED_SCALE = 2.5
EXPERT_BLOCK = 512
LN_EPS = 1e-5
ALPHA = (2 * DEPTH) ** 0.25
NEG_INF = -1e30
KV_W = D_KV_HEADS * HEAD_DIM
QKV_WIDTH = 10 * BRANCH_WIDTH + 2 * KV_W
MOBA_BIAS_TILES = 8
QA, KA, VA = range(3)
PROJ_A_WIDTH = 3 * BRANCH_WIDTH
QB, KB, VB, QC, KC, VC, QD = range(7)
PROJ_R_WIDTH = 8 * BRANCH_WIDTH
KD_COL = 7 * BRANCH_WIDTH // KV_W
VD_COL = KD_COL + 1
F32_EXP_ZERO_BELOW = -104.0

VMEM_LIMIT_BYTES = 56 * 1024 * 1024

F32 = jnp.float32
BF16 = jnp.bfloat16


def _params(semantics, vmem=VMEM_LIMIT_BYTES):
    return pltpu.CompilerParams(dimension_semantics=semantics, vmem_limit_bytes=vmem)


def _dot(a, b):
    return jnp.dot(a, b, preferred_element_type=F32)


def _dot_nt(a, b):
    return lax.dot_general(a, b, (((1,), (1,)), ((), ())), preferred_element_type=F32)


def _mm_kernel(x_ref, w_ref, o_ref):
    o_ref[...] = _dot(x_ref[...], w_ref[...]).astype(o_ref.dtype)


def _matmul(x, w, out_dtype, tm, tn):
    m, k = x.shape
    n = w.shape[1]
    return pl.pallas_call(
        _mm_kernel,
        grid=(m // tm, n // tn),
        in_specs=[pl.BlockSpec((tm, k), lambda i, j: (i, 0)),
                  pl.BlockSpec((k, tn), lambda i, j: (0, j))],
        out_specs=pl.BlockSpec((tm, tn), lambda i, j: (i, j)),
        out_shape=jax.ShapeDtypeStruct((m, n), out_dtype),
        compiler_params=_params(("parallel", "arbitrary")),
        name="qkv_proj",
    )(x, w)


def _t5_bucket(dist):
    n = np.maximum(dist, 0)
    nf = np.maximum(n, 1).astype(np.float32)
    ratio = np.log(nf / np.float32(REL_MAX_EXACT)) / np.float32(math.log(REL_MAX_DIST / REL_MAX_EXACT))
    large = REL_MAX_EXACT + (ratio * np.float32(REL_BUCKETS - REL_MAX_EXACT)).astype(np.int32)
    return np.where(n < REL_MAX_EXACT, n, np.minimum(large, REL_BUCKETS - 1)).astype(np.int32)


def _bias_lookup(tab, bucket, keep):
    bk = jnp.asarray(np.where(keep, bucket, -1), jnp.int32)[None]
    tab_t = tab.astype(F32).T
    out = jnp.full((tab.shape[1],) + bucket.shape, NEG_INF, F32)
    for bkt in range(REL_BUCKETS):
        out = jnp.where(bk == bkt, tab_t[:, bkt].reshape((-1,) + (1,) * bucket.ndim), out)
    return out


def _band_bias(tab, max_steps, dist_scale):
    qi = np.arange(BAND_BLOCK)[:, None]
    kj = np.arange(2 * BAND_BLOCK)[None, :]
    dist = BAND_BLOCK + qi - kj
    return _bias_lookup(tab, _t5_bucket(dist * dist_scale), (dist >= 0) & (dist <= max_steps))


def _moba_bias(tab):
    r = np.arange(MOBA_BLOCK)[:, None]
    c = np.arange(MOBA_BLOCK)[None, :]
    dist = np.stack([dl * MOBA_BLOCK + r - c for dl in range(MOBA_BIAS_TILES)])
    return _bias_lookup(tab, _t5_bucket(dist), dist >= 0)


def _expand_kv(t, kv_rep):
    assert t.shape[1] == 2 * HEAD_DIM and kv_rep * 2 == MIX_HEADS
    tf = t.astype(F32)
    swapped = pltpu.roll(tf, HEAD_DIM, axis=1)
    first = lax.broadcasted_iota(jnp.int32, (1, 2 * HEAD_DIM), 1) < HEAD_DIM
    g0 = jnp.where(first, tf, swapped).astype(t.dtype)
    g1 = jnp.where(first, swapped, tf).astype(t.dtype)
    return jnp.concatenate([g0] * (kv_rep // 2) + [g1] * (kv_rep // 2), axis=1)


def _band_kernel(*refs, kv_rep, with_sink):
    if with_sink:
        q_ref, kp_ref, ko_ref, vp_ref, vo_ref, bias_ref, sink_ref, o_ref = refs
    else:
        q_ref, kp_ref, ko_ref, vp_ref, vo_ref, bias_ref, o_ref, lse_ref = refs
    has_prev = pl.program_id(2) > 0
    blk = BAND_BLOCK
    q = q_ref[0]
    lane = lax.broadcasted_iota(jnp.int32, (1, q.shape[1]), 1)
    in_head = [(lane >= h * HEAD_DIM) & (lane < (h + 1) * HEAD_DIM) for h in range(MIX_HEADS)]
    zero = jnp.zeros_like(q)
    q_stack = jnp.concatenate([jnp.where(in_head[h], q, zero) for h in range(MIX_HEADS)], axis=0)
    q_stack = q_stack * (HEAD_DIM ** -0.5)
    k = jnp.concatenate([kp_ref[0], ko_ref[0]], axis=0)
    v = jnp.concatenate([vp_ref[0], vo_ref[0]], axis=0)
    if kv_rep > 1:
        k, v = _expand_kv(k, kv_rep), _expand_kv(v, kv_rep)
    s = _dot_nt(q_stack, k) + bias_ref[...].reshape(MIX_HEADS * blk, 2 * blk)
    own_cols = lax.broadcasted_iota(jnp.int32, (1, 2 * blk), 1) >= blk
    s = jnp.where(jnp.logical_or(own_cols, has_prev), s, NEG_INF)
    m = s.max(-1, keepdims=True)
    p = jnp.exp(s - m)
    den = p.sum(-1, keepdims=True)
    o_all = _dot(p.astype(BF16), v) / den
    lse_all = m + jnp.log(den)
    o = o_all[:blk]
    lse = jnp.broadcast_to(lse_all[:blk], o.shape)
    for h in range(1, MIX_HEADS):
        rows = slice(h * blk, (h + 1) * blk)
        o = jnp.where(in_head[h], o_all[rows], o)
        lse = jnp.where(in_head[h], lse_all[rows], lse)
    if with_sink:
        o_ref[0] = (o * jax.nn.sigmoid(lse - sink_ref[...])).astype(o_ref.dtype)
    else:
        o_ref[0] = o.astype(o_ref.dtype)
        lse_ref[0] = lse


def _band_attention(proj, qcol, kcol, vcol, kvw, bias, dil, sink_row=None):
    b, s, pw = proj.shape
    qw = BRANCH_WIDTH
    l = s // dil
    nb = l // BAND_BLOCK
    kv_rep = qw // kvw
    pv = proj.reshape(b, l, dil * pw)
    qpr, kpr = pw // qw, pw // kvw
    own = lambda bi, r, n: (bi, n, r)
    q_own = lambda bi, r, n: (bi, n, r * qpr + qcol)
    k_own = lambda bi, r, n: (bi, n, r * kpr + kcol)
    k_prev = lambda bi, r, n: (bi, jnp.maximum(n - 1, 0), r * kpr + kcol)
    v_own = lambda bi, r, n: (bi, n, r * kpr + vcol)
    v_prev = lambda bi, r, n: (bi, jnp.maximum(n - 1, 0), r * kpr + vcol)
    in_specs = [pl.BlockSpec((1, BAND_BLOCK, qw), q_own),
                pl.BlockSpec((1, BAND_BLOCK, kvw), k_prev),
                pl.BlockSpec((1, BAND_BLOCK, kvw), k_own),
                pl.BlockSpec((1, BAND_BLOCK, kvw), v_prev),
                pl.BlockSpec((1, BAND_BLOCK, kvw), v_own),
                pl.BlockSpec((MIX_HEADS, BAND_BLOCK, 2 * BAND_BLOCK), lambda bi, r, n: (0, 0, 0))]
    args = [pv, pv, pv, pv, pv, bias]
    with_sink = sink_row is not None
    if with_sink:
        in_specs.append(pl.BlockSpec((1, qw), lambda bi, r, n: (0, 0)))
        args.append(sink_row)
        out_shape = jax.ShapeDtypeStruct((b, l, dil * qw), BF16)
        out_specs = pl.BlockSpec((1, BAND_BLOCK, qw), own)
    else:
        out_shape = (jax.ShapeDtypeStruct((b, l, dil * qw), F32),) * 2
        out_specs = (pl.BlockSpec((1, BAND_BLOCK, qw), own),) * 2
    out = pl.pallas_call(
        functools.partial(_band_kernel, kv_rep=kv_rep, with_sink=with_sink),
        grid=(b, dil, nb),
        in_specs=in_specs, out_specs=out_specs, out_shape=out_shape,
        compiler_params=_params(("parallel", "parallel", "arbitrary")),
        name="band_attn_sink" if with_sink else f"band_attn_d{dil}",
    )(*args)
    if with_sink:
        return out.reshape(b, s, qw)
    return out[0].reshape(b, s, qw), out[1].reshape(b, s, qw)


def _mix_combine_kernel(o1, l1, o2, l2, o3, l3, out_ref):
    a, bq, c = l1[...], l2[...], l3[...]
    m = jnp.maximum(jnp.maximum(a, bq), c)
    ea, eb, ec = jnp.exp(a - m), jnp.exp(bq - m), jnp.exp(c - m)
    den = ea + eb + ec
    out_ref[...] = (ea / den * o1[...] + eb / den * o2[...] + ec / den * o3[...]).astype(out_ref.dtype)


def _mix_combine(parts, tm=512):
    n, w = parts[0].shape
    spec = pl.BlockSpec((tm, w), lambda i: (i, 0))
    return pl.pallas_call(
        _mix_combine_kernel, grid=(n // tm,),
        in_specs=[spec] * 6, out_specs=spec,
        out_shape=jax.ShapeDtypeStruct((n, w), BF16),
        compiler_params=_params(("parallel",)),
        name="dilated_combine",
    )(*parts)


def _moba_kernel(q_ref, k_ref, v_ref, bias_ref, o_ref, kmean_ref, *, nbk):
    i = pl.program_id(2)
    blk = MOBA_BLOCK
    pair_w = 2 * HEAD_DIM

    nslot = kmean_ref.shape[0]
    assert nbk <= nslot

    @pl.when(i == 0)
    def _():
        kk = k_ref[0].astype(F32).reshape(nbk, blk, pair_w)
        kmean_ref[...] = jnp.zeros_like(kmean_ref)
        kmean_ref[0:nbk, :] = kk.sum(axis=1) * (1.0 / blk)

    q = q_ref[0]
    lane = lax.broadcasted_iota(jnp.int32, (1, pair_w), 1)
    zero = jnp.zeros_like(q)
    qs = jnp.concatenate([jnp.where(lane < HEAD_DIM, q, zero), jnp.where(lane >= HEAD_DIM, q, zero)], axis=0)
    qsc = qs * (HEAD_DIM ** -0.5)
    col = lax.broadcasted_iota(jnp.int32, (2 * blk, nslot), 1)
    gate = jnp.where(col < i, _dot_nt(qs, kmean_ref[...].astype(BF16)), NEG_INF)
    sel = jnp.zeros((2 * blk, nslot), F32)
    for _ in range(MOBA_TOPK):
        mx = gate.max(-1, keepdims=True)
        first = jnp.where(gate == mx, col, nslot).min(-1, keepdims=True)
        pick = col == first
        sel = jnp.where(pick & (col < i), 1.0, sel)
        gate = jnp.where(pick, -jnp.inf, gate)

    def bias_tile(dl):
        dl = jnp.clip(dl, 0, MOBA_BIAS_TILES - 1)
        return jnp.concatenate([bias_ref[0, dl], bias_ref[1, dl]], axis=0)

    own = pl.ds(pl.multiple_of(i * blk, blk), blk)
    s = _dot_nt(qsc, k_ref[0, own, :]) + bias_tile(0)
    m0 = s.max(-1, keepdims=True)
    p = jnp.exp(s - m0)
    l0 = p.sum(-1, keepdims=True)
    acc0 = _dot(p.astype(BF16), v_ref[0, own, :])
    q_aug = jnp.concatenate([qsc, jnp.where(sel > 0.0, 0.0, NEG_INF).astype(BF16)], axis=1)
    tag_of = (lax.broadcasted_iota(jnp.int32, (2 * blk, nslot), 1)
              - lax.broadcasted_iota(jnp.int32, (2 * blk, nslot), 0) // blk)

    def chunk_rows(c):
        return pl.ds(pl.multiple_of(2 * c * blk, 2 * blk), 2 * blk)

    def scores(c):
        ja = 2 * c
        key_tag = jnp.where(tag_of == ja, 1.0, 0.0).astype(BF16)
        bias = jnp.concatenate([bias_tile(i - ja), bias_tile(i - ja - 1)], axis=1)
        return _dot_nt(q_aug, jnp.concatenate([k_ref[0, chunk_rows(c), :], key_tag], axis=1)) + bias

    def body(c, carry):
        m, l, acc = carry
        s = scores(c)
        m_new = jnp.maximum(m, s.max(-1, keepdims=True))
        scale = jnp.exp(m - m_new)
        p = jnp.exp(s - m_new)
        l = scale * l + p.sum(-1, keepdims=True)
        acc = scale * acc + _dot(p.astype(BF16), v_ref[0, chunk_rows(c), :])
        return m_new, l, acc

    _, l, acc = lax.fori_loop(0, (i + 1) // 2, body, (m0, l0, acc0))
    o = acc / l
    o_ref[0] = jnp.where(lane < HEAD_DIM, o[:blk], o[blk:]).astype(o_ref.dtype)


def _moba_attention(proj, bias):
    b, s, pw = proj.shape
    nbk = s // MOBA_BLOCK
    pair_w = 2 * HEAD_DIM
    npair = BRANCH_WIDTH // pair_w
    slab = BRANCH_WIDTH // pair_w
    return pl.pallas_call(
        functools.partial(_moba_kernel, nbk=nbk),
        grid=(b, npair, nbk),
        in_specs=[pl.BlockSpec((1, MOBA_BLOCK, pair_w), lambda bi, p, i: (bi, i, QB * slab + p)),
                  pl.BlockSpec((1, s, pair_w), lambda bi, p, i: (bi, 0, KB * slab + p)),
                  pl.BlockSpec((1, s, pair_w), lambda bi, p, i: (bi, 0, VB * slab + p)),
                  pl.BlockSpec((2, MOBA_BIAS_TILES, MOBA_BLOCK, MOBA_BLOCK), lambda bi, p, i: (p, 0, 0, 0))],
        out_specs=pl.BlockSpec((1, MOBA_BLOCK, pair_w), lambda bi, p, i: (bi, i, p)),
        out_shape=jax.ShapeDtypeStruct((b, s, BRANCH_WIDTH), BF16),
        scratch_shapes=[pltpu.VMEM((pair_w, pair_w), F32)],
        compiler_params=_params(("parallel", "parallel", "arbitrary")),
        name="moba_attn",
    )(proj, proj, proj, bias)


def _sb_kernel(q_ref, k_ref, v_ref, o_ref, kmax_ref):
    i = pl.program_id(1)
    blk = SB_BLOCK
    hd = SB_HEAD_DIM
    s_len = k_ref.shape[1]
    scale = hd ** -0.5

    @pl.when(i == 0)
    def _():
        chunk = 512
        for h in range(SB_HEADS):
            def norm_max(c, mx):
                kk = k_ref[0, pl.ds(pl.multiple_of(c * chunk, chunk), chunk), h * hd:(h + 1) * hd].astype(F32)
                return jnp.maximum(mx, (kk * kk).sum(-1, keepdims=True).max(0, keepdims=True))
            mx = lax.fori_loop(0, s_len // chunk, norm_max, jnp.zeros((1, 1), F32))
            kmax_ref[h] = jnp.broadcast_to(jnp.sqrt(mx), kmax_ref.shape[1:])

    row = lax.broadcasted_iota(jnp.int32, (blk, blk), 0)
    colb = lax.broadcasted_iota(jnp.int32, (blk, blk), 1)
    strictly_past = colb < row
    r2 = lax.broadcasted_iota(jnp.int32, (blk, 2 * blk), 0)
    c2 = lax.broadcasted_iota(jnp.int32, (blk, 2 * blk), 1)
    tri_ones = jnp.where((c2 >= blk) | (r2 >= c2), 1.0, 0.0).astype(BF16)
    qs = [q_ref[0, :, h * hd:(h + 1) * hd] for h in range(SB_HEADS)]
    z_bound = []
    for h in range(SB_HEADS):
        qf = qs[h].astype(F32)
        z_bound.append(jnp.sqrt((qf * qf).sum(-1, keepdims=True)) * kmax_ref[h][0:1, 0:1] * scale)

    def block(h, j, carry, acc, diagonal):
        rows = pl.ds(pl.multiple_of(j * blk, blk), blk)
        kj = k_ref[0, rows, h * hd:(h + 1) * hd]
        vj = v_ref[0, rows, h * hd:(h + 1) * hd]
        z = _dot_nt(qs[h], kj) * scale
        lk = -(jnp.maximum(z, 0.0) + jnp.log1p(jnp.exp(-jnp.abs(z))))
        if diagonal:
            lk = jnp.where(strictly_past, lk, 0.0)
        sums = _dot(lk.astype(BF16), tri_ones)
        log_a = sums[:, :blk] + carry + z
        if diagonal:
            log_a = jnp.where(strictly_past, log_a, NEG_INF)
        acc = acc + _dot(jnp.exp(log_a).astype(BF16), vj)
        return carry + sums[:, blk:], acc

    def all_dead(carries):
        worst = None
        for h in range(SB_HEADS):
            w = (carries[h] * 0.98 + z_bound[h] * 1.02).max()
            worst = w if worst is None else jnp.maximum(worst, w)
        return worst < F32_EXP_ZERO_BELOW - 6.0

    state = [block(h, i, jnp.zeros((blk, blk), F32), jnp.zeros((blk, hd), F32), True) for h in range(SB_HEADS)]
    carries = tuple(st[0] for st in state)
    accs = tuple(st[1] for st in state)

    def cond(c):
        t, dead, _, _ = c
        return (t < i) & jnp.logical_not(dead)

    def body(c):
        t, _, carries, accs = c
        new = [block(h, i - 1 - t, carries[h], accs[h], False) for h in range(SB_HEADS)]
        carries = tuple(st[0] for st in new)
        return t + 1, all_dead(carries), carries, tuple(st[1] for st in new)

    _, _, _, accs = lax.while_loop(cond, body, (jnp.int32(0), all_dead(carries), carries, accs))
    for h in range(SB_HEADS):
        o_ref[0, :, h * hd:(h + 1) * hd] = accs[h].astype(o_ref.dtype)


def _sb_attention(proj):
    b, s, pw = proj.shape
    nq = s // SB_BLOCK
    w = BRANCH_WIDTH
    return pl.pallas_call(
        _sb_kernel,
        grid=(b, nq),
        in_specs=[pl.BlockSpec((1, SB_BLOCK, w), lambda bi, i: (bi, i, QC)),
                  pl.BlockSpec((1, s, w), lambda bi, i: (bi, 0, KC)),
                  pl.BlockSpec((1, s, w), lambda bi, i: (bi, 0, VC))],
        out_specs=pl.BlockSpec((1, SB_BLOCK, w), lambda bi, i: (bi, i, 0)),
        out_shape=jax.ShapeDtypeStruct((b, s, w), BF16),
        scratch_shapes=[pltpu.VMEM((SB_HEADS, 8, 128), F32)],
        compiler_params=_params(("parallel", "arbitrary")),
        name="stick_breaking_attn",
    )(proj, proj, proj)


def _merge_kernel(x_ref, oa_ref, ob_ref, oc_ref, od_ref, g0, g1, g2, g3, w0, w1, w2, w3, out_ref):
    x = x_ref[...]
    acc = None
    for o_ref, wg_ref, wo_ref in ((oa_ref, g0, w0), (ob_ref, g1, w1), (oc_ref, g2, w2), (od_ref, g3, w3)):
        gate = jax.nn.sigmoid(_dot(x, wg_ref[...]))
        term = gate * _dot(o_ref[...], wo_ref[...])
        acc = term if acc is None else acc + term
    out_ref[...] = acc.astype(out_ref.dtype)


def _merge(x_bf, branches, w_gates, w_o, tm=512, tn=512):
    n, d = x_bf.shape
    nj = d // tn
    row = lambda j, i: (i, 0)
    in_specs = [pl.BlockSpec((tm, d), row)] + [pl.BlockSpec((tm, BRANCH_WIDTH), row)] * N_BRANCHES
    in_specs += [pl.BlockSpec((d, tn), functools.partial(lambda j, i, br: (0, br * nj + j), br=br))
                 for br in range(N_BRANCHES)]
    in_specs += [pl.BlockSpec((BRANCH_WIDTH, tn), functools.partial(lambda j, i, br: (br, j), br=br))
                 for br in range(N_BRANCHES)]
    return pl.pallas_call(
        _merge_kernel,
        grid=(nj, n // tm),
        in_specs=in_specs,
        out_specs=pl.BlockSpec((tm, tn), lambda j, i: (i, j)),
        out_shape=jax.ShapeDtypeStruct((n, d), BF16),
        compiler_params=_params(("parallel", "arbitrary")),
        name="gated_merge",
    )(x_bf, *branches, *([w_gates] * N_BRANCHES), *([w_o] * N_BRANCHES))


def _layer_norm(r, g, b):
    mu = r.mean(-1, keepdims=True)
    var = jnp.square(r - mu).mean(-1, keepdims=True)
    return (r - mu) * lax.rsqrt(var + LN_EPS) * g + b


def _out_ln_kernel(m_ref, w_ref, x_ref, g_ref, b_ref, wrt_ref, xo_ref, xb_ref, lgt_ref):
    y = _dot(m_ref[...], w_ref[...])
    xn = _layer_norm(ALPHA * x_ref[...] + y, g_ref[...], b_ref[...])
    xo_ref[...] = xn
    xb = xn.astype(BF16)
    xb_ref[...] = xb
    lgt_ref[...] = _dot_nt(wrt_ref[...], xb)


def _out_ln(merged, w_out, x, g, b, w_router_t, tm=512):
    n, d = x.shape
    e = w_router_t.shape[0]
    row = lambda i: (i, 0)
    fix = lambda i: (0, 0)
    return pl.pallas_call(
        _out_ln_kernel,
        grid=(n // tm,),
        in_specs=[pl.BlockSpec((tm, d), row), pl.BlockSpec((d, d), fix), pl.BlockSpec((tm, d), row),
                  pl.BlockSpec((1, d), fix), pl.BlockSpec((1, d), fix), pl.BlockSpec((e, d), fix)],
        out_specs=(pl.BlockSpec((tm, d), row), pl.BlockSpec((tm, d), row), pl.BlockSpec((e, tm), lambda i: (0, i))),
        out_shape=(jax.ShapeDtypeStruct((n, d), F32), jax.ShapeDtypeStruct((n, d), BF16),
                   jax.ShapeDtypeStruct((e, n), F32)),
        compiler_params=_params(("parallel",)),
        name="out_proj_ln_router",
    )(merged, w_out, x, g, b, w_router_t)


def _route_kernel(lg_ref, rb_ref, su_ref, dest_ref, gw_ref, cnt_ref, counts_ref, run_ref, pstart_ref, *, tb):
    p = pl.program_id(0)
    i = pl.program_id(1)
    ne, tm = lg_ref.shape
    gsz = ne // N_EXPERT_GROUPS
    scores = jax.nn.sigmoid(lg_ref[...])
    biased = scores + rb_ref[...]
    io_in = lax.broadcasted_iota(jnp.int32, (gsz, tm), 0)
    groups = [biased[g * gsz:(g + 1) * gsz, :] for g in range(N_EXPERT_GROUPS)]
    gscore = []
    for rows in groups:
        m1 = rows.max(0, keepdims=True)
        i1 = jnp.where(rows == m1, io_in, gsz).min(0, keepdims=True)
        m2 = jnp.where(io_in == i1, -jnp.inf, rows).max(0, keepdims=True)
        gscore.append(m1 + m2)
    masked = []
    for g in range(N_EXPERT_GROUPS):
        beaten = jnp.zeros((1, tm), F32)
        for o in range(N_EXPERT_GROUPS):
            if o == g:
                continue
            wins = (gscore[o] >= gscore[g]) if o < g else (gscore[o] > gscore[g])
            beaten = beaten + jnp.where(wins, 1.0, 0.0)
        masked.append(jnp.where(beaten < TOPK_GROUPS, groups[g], NEG_INF))
    cur = jnp.concatenate(masked, axis=0)
    io_e = lax.broadcasted_iota(jnp.int32, (ne, tm), 0)
    sel = jnp.zeros((ne, tm), F32)
    for _ in range(TOP_K):
        mx = cur.max(0, keepdims=True)
        first = jnp.where(cur == mx, io_e, ne).min(0, keepdims=True)
        pick = io_e == first
        sel = jnp.where(pick, 1.0, sel)
        cur = jnp.where(pick, -jnp.inf, cur)
    per_expert = sel.sum(1, keepdims=True)

    @pl.when((p == 0) & (i == 0))
    def _():
        counts_ref[...] = jnp.zeros_like(counts_ref)

    @pl.when(p == 0)
    def _():
        counts_ref[...] += per_expert
        dest_ref[...] = jnp.zeros_like(dest_ref)
        gw_ref[...] = jnp.zeros_like(gw_ref)
        cnt_ref[...] = jnp.zeros_like(cnt_ref)

    @pl.when((p == 1) & (i == 0))
    def _():
        padded = jnp.floor((counts_ref[...] + (tb - 1)) / tb) * tb
        r = lax.broadcasted_iota(jnp.int32, (ne, ne), 0)
        c = lax.broadcasted_iota(jnp.int32, (ne, ne), 1)
        padded_l = jnp.where(r == c, padded, 0.0).sum(0, keepdims=True)
        pstart_ref[...] = jnp.where(c < r, padded_l, 0.0).sum(1, keepdims=True)
        run_ref[...] = jnp.zeros_like(run_ref)

    @pl.when(p == 1)
    def _():
        sel_bf = sel.astype(BF16)
        slot = pstart_ref[...] + run_ref[...] + _dot(sel_bf, su_ref[...])
        run_ref[...] += per_expert
        r = lax.broadcasted_iota(jnp.int32, (ne, ne), 0)
        c = lax.broadcasted_iota(jnp.int32, (ne, ne), 1)
        order = _dot(jnp.where(c < r, 1.0, 0.0).astype(BF16), sel_bf)
        picked = sel * scores
        gwm = picked / picked.sum(0, keepdims=True) * ROUTED_SCALE
        for k in range(TOP_K):
            pk = (sel > 0.0) & (order == k)
            dest_ref[k:k + 1, :] = jnp.where(pk, slot, 0.0).sum(0, keepdims=True).astype(jnp.int32)
            gw_ref[k:k + 1, :] = jnp.where(pk, gwm, 0.0).sum(0, keepdims=True)
        cnt_ref[...] = jnp.broadcast_to(counts_ref[...], cnt_ref.shape)


def _route(logits_t, router_bias, tb, tm=512):
    ne, n = logits_t.shape
    strictly_earlier = jnp.asarray(np.triu(np.ones((tm, tm), np.float32), 1), BF16)
    fix = lambda p, i: (0, 0)
    out_blk = lambda p, i: (0, i * p)
    dest, gw, cnt = pl.pallas_call(
        functools.partial(_route_kernel, tb=tb),
        grid=(2, n // tm),
        in_specs=[pl.BlockSpec((ne, tm), lambda p, i: (0, i)), pl.BlockSpec((ne, 1), fix),
                  pl.BlockSpec((tm, tm), fix)],
        out_specs=(pl.BlockSpec((TOP_K, tm), out_blk), pl.BlockSpec((TOP_K, tm), out_blk),
                   pl.BlockSpec((ne, 128), fix)),
        out_shape=(jax.ShapeDtypeStruct((TOP_K, n), jnp.int32), jax.ShapeDtypeStruct((TOP_K, n), F32),
                   jax.ShapeDtypeStruct((ne, 128), F32)),
        scratch_shapes=[pltpu.VMEM((ne, 1), F32)] * 3,
        compiler_params=_params(("arbitrary", "arbitrary")),
        name="router_slots",
    )(logits_t, router_bias.astype(F32)[:, None], strictly_earlier)
    return dest, gw, cnt[:, 0]


def _expert_kernel(be_ref, used_ref, xs_ref, wg_ref, wu_ref, wd_ref, y_ref, wg_bf, wu_bf, wd_bf):
    i = pl.program_id(0)

    @pl.when((i == 0) | (be_ref[i] != be_ref[jnp.maximum(i - 1, 0)]))
    def _():
        wg_bf[...] = wg_ref[0].astype(BF16)
        wu_bf[...] = wu_ref[0].astype(BF16)
        wd_bf[...] = wd_ref[0].astype(BF16)

    @pl.when(used_ref[i] > 0)
    def _():
        xs = xs_ref[...]
        h = jax.nn.silu(_dot(xs, wg_bf[...])) * _dot(xs, wu_bf[...])
        y_ref[...] = _dot(h.astype(BF16), wd_bf[...]).astype(y_ref.dtype)


def _experts(xs, blk_expert, blk_used, wg, wu, wd, tb):
    ns, d = xs.shape
    hdn = wg.shape[2]
    grid_spec = pltpu.PrefetchScalarGridSpec(
        num_scalar_prefetch=2, grid=(ns // tb,),
        in_specs=[pl.BlockSpec((tb, d), lambda i, be, us: (i, 0)),
                  pl.BlockSpec((1, d, hdn), lambda i, be, us: (be[i], 0, 0)),
                  pl.BlockSpec((1, d, hdn), lambda i, be, us: (be[i], 0, 0)),
                  pl.BlockSpec((1, hdn, d), lambda i, be, us: (be[i], 0, 0))],
        out_specs=pl.BlockSpec((tb, d), lambda i, be, us: (i, 0)),
        scratch_shapes=[pltpu.VMEM((d, hdn), BF16), pltpu.VMEM((d, hdn), BF16), pltpu.VMEM((hdn, d), BF16)])
    return pl.pallas_call(
        _expert_kernel, grid_spec=grid_spec,
        out_shape=jax.ShapeDtypeStruct((ns, d), BF16),
        compiler_params=_params(("arbitrary",)),
        name="grouped_swiglu",
    )(blk_expert, blk_used, xs, wg, wu, wd)


def _final_ln_kernel(x_ref, sh_ref, ys_ref, gw_ref, g_ref, b_ref, xo_ref, xb_ref):
    ffn = sh_ref[...].astype(F32)
    gw = gw_ref[...]
    for kk in range(TOP_K):
        ffn = ffn + gw[:, kk:kk + 1] * ys_ref[kk].astype(F32)
    xn = _layer_norm(ALPHA * x_ref[...] + ffn, g_ref[...], b_ref[...])
    xo_ref[...] = xn
    xb_ref[...] = xn.astype(BF16)


def _final_ln(x, shared, ys_k, gw, g, b, tm=256):
    n, d = x.shape
    row = lambda i: (i, 0)
    fix = lambda i: (0, 0)
    return pl.pallas_call(
        _final_ln_kernel,
        grid=(n // tm,),
        in_specs=[pl.BlockSpec((tm, d), row), pl.BlockSpec((tm, d), row),
                  pl.BlockSpec((TOP_K, tm, d), lambda i: (0, i, 0)), pl.BlockSpec((tm, TOP_K), row),
                  pl.BlockSpec((1, d), fix), pl.BlockSpec((1, d), fix)],
        out_specs=(pl.BlockSpec((tm, d), row), pl.BlockSpec((tm, d), row)),
        out_shape=(jax.ShapeDtypeStruct((n, d), F32), jax.ShapeDtypeStruct((n, d), BF16)),
        compiler_params=_params(("parallel",)),
        name="moe_combine_ln",
    )(x, shared, ys_k, gw, g, b)


def _layer(x, x_bf, biases, w_in, w_o, w_out, sinks, ln1_g, ln1_b, w_router, router_bias,
           w_gate, w_up, w_down, w_sh_gate, w_sh_up, w_sh_down, ln2_g, ln2_b, b, s):
    n, d = x.shape
    bias_a, bias_b, bias_d = biases
    bw = BRANCH_WIDTH
    w_a = w_in[:, :PROJ_A_WIDTH].astype(BF16)
    w_r = jnp.pad(w_in[:, PROJ_A_WIDTH:QKV_WIDTH].astype(BF16),
                  ((0, 0), (0, PROJ_R_WIDTH - (QKV_WIDTH - PROJ_A_WIDTH))))
    w_gates = w_in[:, QKV_WIDTH:].astype(BF16)
    proj_a = _matmul(x_bf, w_a, BF16, tm=min(1024, n), tn=bw).reshape(b, s, PROJ_A_WIDTH)
    proj = _matmul(x_bf, w_r, BF16, tm=min(1024, n), tn=bw).reshape(b, s, PROJ_R_WIDTH)

    parts = []
    for dil, ba in zip(A_DILATIONS, bias_a):
        o, lse = _band_attention(proj_a, QA, KA, VA, bw, ba, dil)
        parts += [o.reshape(n, bw), lse.reshape(n, bw)]
    o_a = _mix_combine(parts)
    o_b = _moba_attention(proj, bias_b).reshape(n, bw)
    o_c = _sb_attention(proj).reshape(n, bw)
    sink_row = jnp.repeat(sinks.astype(F32), HEAD_DIM)[None, :]
    o_d = _band_attention(proj, QD, KD_COL, VD_COL, KV_W, bias_d, 1, sink_row).reshape(n, bw)

    merged = _merge(x_bf, (o_a, o_b, o_c, o_d), w_gates, w_o.astype(BF16))
    x1, x1_bf, logits_t = _out_ln(merged, w_out.astype(BF16), x, ln1_g[None, :], ln1_b[None, :],
                                  w_router.T.astype(BF16))

    tb = EXPERT_BLOCK
    dest, gw, counts = _route(logits_t, router_bias, tb)
    nblk = -(-n * TOP_K // tb) + N_EXPERTS
    pends = jnp.cumsum(jnp.ceil(counts / tb) * tb).astype(jnp.int32)
    blk_start = jnp.arange(nblk, dtype=jnp.int32) * tb
    blk_expert = jnp.minimum((pends[None, :] <= blk_start[:, None]).sum(1), N_EXPERTS - 1).astype(jnp.int32)
    blk_used = (blk_start < pends[-1]).astype(jnp.int32)
    token = jnp.broadcast_to(jnp.arange(n, dtype=jnp.int32)[None, :], dest.shape)
    slot_token = (jnp.arange(nblk * tb, dtype=jnp.int32) % n).at[dest.reshape(-1)].set(
        token.reshape(-1), unique_indices=True)
    xs = x1_bf[slot_token]
    ys = _experts(xs, blk_expert, blk_used, w_gate, w_up, w_down, tb)
    nsh = n // tb
    shared = _experts(x1_bf, jnp.zeros((nsh,), jnp.int32), jnp.ones((nsh,), jnp.int32),
                      w_sh_gate[None], w_sh_up[None], w_sh_down[None], tb)
    return _final_ln(x1, shared, ys[dest], gw.T, ln2_g[None, :], ln2_b[None, :])


def kernel(x, rel_bias, w_in, w_o, w_out, attn_sinks, ln1_g, ln1_b, w_router, router_bias,
           w_gate, w_up, w_down, w_sh_gate, w_sh_up, w_sh_down, ln2_g, ln2_b):
    b, s, d = x.shape
    n = b * s
    h = MIX_HEADS
    bias_a = [_band_bias(rel_bias[:, :h], win // dil, dil) for win, dil in zip(A_WINDOWS, A_DILATIONS)]
    bias_b = _moba_bias(rel_bias[:, h:2 * h])
    bias_d = _band_bias(rel_bias[:, 2 * h:], SWA_WINDOW - 1, 1)
    xf = x.reshape(n, d)
    xb = xf.astype(BF16)
    for l in range(w_in.shape[0]):
        xf, xb = _layer(xf, xb, (bias_a, bias_b, bias_d), w_in[l], w_o[l], w_out[l], attn_sinks[l],
                        ln1_g[l], ln1_b[l], w_router[l], router_bias[l], w_gate[l], w_up[l], w_down[l],
                        w_sh_gate[l], w_sh_up[l], w_sh_down[l], ln2_g[l], ln2_b[l], b, s)
    return xf.reshape(b, s, d)
```

```python
import functools
import math

import numpy as np
import jax
import jax.numpy as jnp
from jax import lax
from jax.experimental import pallas as pl
from jax.experimental.pallas import tpu as pltpu

D_MODEL = 2048
DEPTH = 4
HEAD_DIM = 64
N_BRANCHES = 4
BRANCH_WIDTH = D_MODEL // N_BRANCHES
MIX_HEADS = BRANCH_WIDTH // HEAD_DIM
SB_HEADS = 4
SB_HEAD_DIM = BRANCH_WIDTH // SB_HEADS
A_WINDOWS = (128, 512, 2048)
A_DILATIONS = (1, 4, 16)
MOBA_BLOCK = 256
MOBA_TOPK = 3
SB_BLOCK = 128
SB_UNROLLED_BLOCKS = 2
SWA_WINDOW = 128
D_KV_HEADS = 2
BAND_BLOCK = 128
BAND_SUBBLOCKS = 2
REL_BUCKETS = 32
REL_MAX_EXACT = 16
REL_MAX_DIST = 2048
N_EXPERTS = 64
TOP_K = 8
N_EXPERT_GROUPS = 8
TOPK_GROUPS = 4
EXPERT_HIDDEN = D_MODEL * 3 // 16
ROUTED_SCALE = 2.5
EXPERT_BLOCK = 512
LN_EPS = 1e-5
ALPHA = (2 * DEPTH) ** 0.25
NEG_INF = -1e30
KV_W = D_KV_HEADS * HEAD_DIM
QKV_WIDTH = 10 * BRANCH_WIDTH + 2 * KV_W
MOBA_BIAS_TILES = 8
QA, KA, VA = range(3)
PROJ_A_WIDTH = 3 * BRANCH_WIDTH
QB, KB, VB, QC, KC, VC, QD = range(7)
PROJ_R_WIDTH = 8 * BRANCH_WIDTH
KD_COL = 7 * BRANCH_WIDTH // KV_W
VD_COL = KD_COL + 1
F32_EXP_ZERO_BELOW = -104.0

VMEM_LIMIT_BYTES = 56 * 1024 * 1024

F32 = jnp.float32
BF16 = jnp.bfloat16


def _params(semantics, vmem=VMEM_LIMIT_BYTES):
    return pltpu.CompilerParams(dimension_semantics=semantics, vmem_limit_bytes=vmem)


def _dot(a, b):
    return jnp.dot(a, b, preferred_element_type=F32)


def _dot_nt(a, b):
    return lax.dot_general(a, b, (((1,), (1,)), ((), ())), preferred_element_type=F32)


def _mm_kernel(x_ref, w_ref, o_ref):
    o_ref[...] = _dot(x_ref[...], w_ref[...]).astype(o_ref.dtype)


def _matmul(x, w, layer, col0, ncols, out_dtype, tm, tn):
    m, k = x.shape
    return pl.pallas_call(
        _mm_kernel,
        grid=(m // tm, ncols),
        in_specs=[pl.BlockSpec((tm, k), lambda i, j: (i, 0)),
                  pl.BlockSpec((k, tn), lambda i, j: (layer, col0 + j))],
        out_specs=pl.BlockSpec((tm, tn), lambda i, j: (i, j)),
        out_shape=jax.ShapeDtypeStruct((m, ncols * tn), out_dtype),
        compiler_params=_params(("parallel", "arbitrary")),
        name="qkv_proj",
    )(x, w)


def _cast_split_kernel(w_ref, qkv_ref, gates_ref):
    w = w_ref[...]
    qkv_ref[...] = w[:, :qkv_ref.shape[1]].astype(BF16)
    gates_ref[...] = w[:, QKV_WIDTH:].astype(BF16)


def _cast_w_in(w_in, rows=128):
    nl, d, in_w = w_in.shape
    lead = PROJ_A_WIDTH + PROJ_R_WIDTH
    return pl.pallas_call(
        _cast_split_kernel,
        grid=(nl * d // rows,),
        in_specs=[pl.BlockSpec((rows, in_w), lambda i: (i, 0))],
        out_specs=(pl.BlockSpec((rows, lead), lambda i: (i, 0)),
                   pl.BlockSpec((rows, in_w - QKV_WIDTH), lambda i: (i, 0))),
        out_shape=(jax.ShapeDtypeStruct((nl * d, lead), BF16),
                   jax.ShapeDtypeStruct((nl * d, in_w - QKV_WIDTH), BF16)),
        compiler_params=_params(("parallel",)),
        name="cast_w_in",
    )(w_in.reshape(nl * d, in_w))


def _t5_bucket(dist):
    n = np.maximum(dist, 0)
    nf = np.maximum(n, 1).astype(np.float32)
    ratio = np.log(nf / np.float32(REL_MAX_EXACT)) / np.float32(math.log(REL_MAX_DIST / REL_MAX_EXACT))
    large = REL_MAX_EXACT + (ratio * np.float32(REL_BUCKETS - REL_MAX_EXACT)).astype(np.int32)
    return np.where(n < REL_MAX_EXACT, n, np.minimum(large, REL_BUCKETS - 1)).astype(np.int32)


def _bias_lookup(tab, bucket, keep):
    bk = jnp.asarray(np.where(keep, bucket, -1), jnp.int32)[None]
    tab_t = tab.astype(F32).T
    out = jnp.full((tab.shape[1],) + bucket.shape, NEG_INF, F32)
    for bkt in range(REL_BUCKETS):
        out = jnp.where(bk == bkt, tab_t[:, bkt].reshape((-1,) + (1,) * bucket.ndim), out)
    return out


def _band_bias(tab, max_steps, dist_scale):
    qi = np.arange(BAND_BLOCK)[:, None]
    kj = np.arange(2 * BAND_BLOCK)[None, :]
    dist = BAND_BLOCK + qi - kj
    return _bias_lookup(tab, _t5_bucket(dist * dist_scale), (dist >= 0) & (dist <= max_steps))


def _moba_bias(tab):
    r = np.arange(MOBA_BLOCK)[:, None]
    c = np.arange(MOBA_BLOCK)[None, :]
    dist = np.stack([dl * MOBA_BLOCK + r - c for dl in range(MOBA_BIAS_TILES)])
    return _bias_lookup(tab, _t5_bucket(dist), dist >= 0)


def _duplicate_kv_heads(t):
    assert t.shape[1] == 2 * HEAD_DIM
    tf = t.astype(F32)
    swapped = pltpu.roll(tf, HEAD_DIM, axis=1)
    first = lax.broadcasted_iota(jnp.int32, (1, 2 * HEAD_DIM), 1) < HEAD_DIM
    return jnp.where(first, tf, swapped).astype(t.dtype), jnp.where(first, swapped, tf).astype(t.dtype)


def _band_kernel(*refs, kv_rep, with_sink):
    if with_sink:
        q_ref, kp_ref, ko_ref, vp_ref, vo_ref, bias_ref, sink_ref, o_ref = refs
    else:
        q_ref, kp_ref, ko_ref, vp_ref, vo_ref, bias_ref, o_ref, lse_ref = refs
    blk = BAND_BLOCK
    pair_w = 2 * HEAD_DIM
    first = lax.broadcasted_iota(jnp.int32, (1, pair_w), 1) < HEAD_DIM
    own_cols = lax.broadcasted_iota(jnp.int32, (1, 2 * blk), 1) >= blk
    k_all = jnp.concatenate([kp_ref[0], ko_ref[0]], axis=0)
    v_all = jnp.concatenate([vp_ref[0], vo_ref[0]], axis=0)
    if kv_rep > 1:
        k_dup, v_dup = _duplicate_kv_heads(k_all), _duplicate_kv_heads(v_all)
    for sb in range(BAND_SUBBLOCKS):
        rows_q = slice(sb * blk, (sb + 1) * blk)
        keys = slice(sb * blk, (sb + 2) * blk)
        q = q_ref[0, rows_q, :] * (HEAD_DIM ** -0.5)
        outs, lses = [], []
        for p2 in range(MIX_HEADS // 2):
            cols = slice(p2 * pair_w, (p2 + 1) * pair_w)
            qp = q[:, cols]
            zero = jnp.zeros_like(qp)
            q2 = jnp.concatenate([jnp.where(first, qp, zero), jnp.where(first, zero, qp)], axis=0)
            if kv_rep > 1:
                g = 2 * p2 // kv_rep
                k, v = k_dup[g][keys], v_dup[g][keys]
            else:
                k, v = k_all[keys, cols], v_all[keys, cols]
            s = _dot_nt(q2, k) + bias_ref[2 * p2:2 * p2 + 2].reshape(2 * blk, 2 * blk)
            if sb == 0:
                s = jnp.where(jnp.logical_or(own_cols, pl.program_id(2) > 0), s, NEG_INF)
            m = s.max(-1, keepdims=True)
            p = jnp.exp(s - m)
            den = p.sum(-1, keepdims=True)
            o2 = _dot(p.astype(BF16), v) / den
            lse2 = m + jnp.log(den)
            outs.append(jnp.where(first, o2[:blk], o2[blk:]))
            lses.append(jnp.where(first, lse2[:blk], lse2[blk:]))
        o = jnp.concatenate(outs, axis=1)
        lse = jnp.concatenate(lses, axis=1)
        if with_sink:
            o_ref[0, rows_q, :] = (o * jax.nn.sigmoid(lse - sink_ref[...])).astype(o_ref.dtype)
        else:
            o_ref[0, rows_q, :] = o.astype(o_ref.dtype)
            lse_ref[0, rows_q, :] = lse


def _band_attention(proj, qcol, kcol, vcol, kvw, bias, dil, sink_row=None):
    b, s, pw = proj.shape
    qw = BRANCH_WIDTH
    l = s // dil
    rows = BAND_SUBBLOCKS * BAND_BLOCK
    nb = l // rows
    kv_rep = qw // kvw
    pv = proj.reshape(b, l, dil * pw)
    qpr, kpr = pw // qw, pw // kvw
    prev_blk = lambda n: jnp.maximum(BAND_SUBBLOCKS * n - 1, 0)
    own = lambda bi, r, n: (bi, n, r)
    q_own = lambda bi, r, n: (bi, n, r * qpr + qcol)
    k_own = lambda bi, r, n: (bi, n, r * kpr + kcol)
    k_prev = lambda bi, r, n: (bi, prev_blk(n), r * kpr + kcol)
    v_own = lambda bi, r, n: (bi, n, r * kpr + vcol)
    v_prev = lambda bi, r, n: (bi, prev_blk(n), r * kpr + vcol)
    in_specs = [pl.BlockSpec((1, rows, qw), q_own),
                pl.BlockSpec((1, BAND_BLOCK, kvw), k_prev),
                pl.BlockSpec((1, rows, kvw), k_own),
                pl.BlockSpec((1, BAND_BLOCK, kvw), v_prev),
                pl.BlockSpec((1, rows, kvw), v_own),
                pl.BlockSpec((MIX_HEADS, BAND_BLOCK, 2 * BAND_BLOCK), lambda bi, r, n: (0, 0, 0))]
    args = [pv, pv, pv, pv, pv, bias]
    with_sink = sink_row is not None
    if with_sink:
        in_specs.append(pl.BlockSpec((1, qw), lambda bi, r, n: (0, 0)))
        args.append(sink_row)
        out_shape = jax.ShapeDtypeStruct((b, l, dil * qw), BF16)
        out_specs = pl.BlockSpec((1, rows, qw), own)
    else:
        out_shape = (jax.ShapeDtypeStruct((b, l, dil * qw), F32),) * 2
        out_specs = (pl.BlockSpec((1, rows, qw), own),) * 2
    out = pl.pallas_call(
        functools.partial(_band_kernel, kv_rep=kv_rep, with_sink=with_sink),
        grid=(b, dil, nb),
        in_specs=in_specs, out_specs=out_specs, out_shape=out_shape,
        compiler_params=_params(("parallel", "parallel", "arbitrary")),
        name="band_attn_sink" if with_sink else f"band_attn_d{dil}",
    )(*args)
    if with_sink:
        return out.reshape(b, s, qw)
    return out[0].reshape(b, s, qw), out[1].reshape(b, s, qw)


def _mix_combine_kernel(o1, l1, o2, l2, o3, l3, out_ref):
    a, bq, c = l1[...], l2[...], l3[...]
    m = jnp.maximum(jnp.maximum(a, bq), c)
    ea, eb, ec = jnp.exp(a - m), jnp.exp(bq - m), jnp.exp(c - m)
    den = ea + eb + ec
    out_ref[...] = (ea / den * o1[...] + eb / den * o2[...] + ec / den * o3[...]).astype(out_ref.dtype)


def _mix_combine(parts, tm=512):
    n, w = parts[0].shape
    spec = pl.BlockSpec((tm, w), lambda i: (i, 0))
    return pl.pallas_call(
        _mix_combine_kernel, grid=(n // tm,),
        in_specs=[spec] * 6, out_specs=spec,
        out_shape=jax.ShapeDtypeStruct((n, w), BF16),
        compiler_params=_params(("parallel",)),
        name="dilated_combine",
    )(*parts)


def _moba_kernel(q_ref, k_ref, v_ref, bias_ref, o_ref, kmean_ref, *, nbk):
    i = pl.program_id(2)
    blk = MOBA_BLOCK
    pair_w = 2 * HEAD_DIM

    nslot = kmean_ref.shape[0]
    assert nbk <= nslot

    @pl.when(i == 0)
    def _():
        kk = k_ref[0].astype(F32).reshape(nbk, blk, pair_w)
        kmean_ref[...] = jnp.zeros_like(kmean_ref)
        kmean_ref[0:nbk, :] = kk.sum(axis=1) * (1.0 / blk)

    q = q_ref[0]
    lane = lax.broadcasted_iota(jnp.int32, (1, pair_w), 1)
    zero = jnp.zeros_like(q)
    qs = jnp.concatenate([jnp.where(lane < HEAD_DIM, q, zero), jnp.where(lane >= HEAD_DIM, q, zero)], axis=0)
    qsc = qs * (HEAD_DIM ** -0.5)
    col = lax.broadcasted_iota(jnp.int32, (2 * blk, nslot), 1)
    gate = jnp.where(col < i, _dot_nt(qs, kmean_ref[...].astype(BF16)), NEG_INF)
    sel = jnp.zeros((2 * blk, nslot), F32)
    for _ in range(MOBA_TOPK):
        mx = gate.max(-1, keepdims=True)
        first = jnp.where(gate == mx, col, nslot).min(-1, keepdims=True)
        pick = col == first
        sel = jnp.where(pick & (col < i), 1.0, sel)
        gate = jnp.where(pick, -jnp.inf, gate)

    def bias_tile(dl):
        dl = jnp.clip(dl, 0, MOBA_BIAS_TILES - 1)
        return jnp.concatenate([bias_ref[0, dl], bias_ref[1, dl]], axis=0)

    own = pl.ds(pl.multiple_of(i * blk, blk), blk)
    s = _dot_nt(qsc, k_ref[0, own, :]) + bias_tile(0)
    m0 = s.max(-1, keepdims=True)
    p = jnp.exp(s - m0)
    l0 = p.sum(-1, keepdims=True)
    acc0 = _dot(p.astype(BF16), v_ref[0, own, :])
    q_aug = jnp.concatenate([qsc, jnp.where(sel > 0.0, 0.0, NEG_INF).astype(BF16)], axis=1)
    tag_of = (lax.broadcasted_iota(jnp.int32, (2 * blk, nslot), 1)
              - lax.broadcasted_iota(jnp.int32, (2 * blk, nslot), 0) // blk)

    def chunk_rows(c):
        return pl.ds(pl.multiple_of(2 * c * blk, 2 * blk), 2 * blk)

    def scores(c):
        ja = 2 * c
        key_tag = jnp.where(tag_of == ja, 1.0, 0.0).astype(BF16)
        bias = jnp.concatenate([bias_tile(i - ja), bias_tile(i - ja - 1)], axis=1)
        return _dot_nt(q_aug, jnp.concatenate([k_ref[0, chunk_rows(c), :], key_tag], axis=1)) + bias

    def body(c, carry):
        m, l, acc = carry
        s = scores(c)
        m_new = jnp.maximum(m, s.max(-1, keepdims=True))
        scale = jnp.exp(m - m_new)
        p = jnp.exp(s - m_new)
        l = scale * l + p.sum(-1, keepdims=True)
        acc = scale * acc + _dot(p.astype(BF16), v_ref[0, chunk_rows(c), :])
        return m_new, l, acc

    _, l, acc = lax.fori_loop(0, (i + 1) // 2, body, (m0, l0, acc0))
    o = acc / l
    o_ref[0] = jnp.where(lane < HEAD_DIM, o[:blk], o[blk:]).astype(o_ref.dtype)


def _moba_attention(proj, bias):
    b, s, pw = proj.shape
    nbk = s // MOBA_BLOCK
    pair_w = 2 * HEAD_DIM
    npair = BRANCH_WIDTH // pair_w
    slab = BRANCH_WIDTH // pair_w
    return pl.pallas_call(
        functools.partial(_moba_kernel, nbk=nbk),
        grid=(b, npair, nbk),
        in_specs=[pl.BlockSpec((1, MOBA_BLOCK, pair_w), lambda bi, p, i: (bi, i, QB * slab + p)),
                  pl.BlockSpec((1, s, pair_w), lambda bi, p, i: (bi, 0, KB * slab + p)),
                  pl.BlockSpec((1, s, pair_w), lambda bi, p, i: (bi, 0, VB * slab + p)),
                  pl.BlockSpec((2, MOBA_BIAS_TILES, MOBA_BLOCK, MOBA_BLOCK), lambda bi, p, i: (p, 0, 0, 0))],
        out_specs=pl.BlockSpec((1, MOBA_BLOCK, pair_w), lambda bi, p, i: (bi, i, p)),
        out_shape=jax.ShapeDtypeStruct((b, s, BRANCH_WIDTH), BF16),
        scratch_shapes=[pltpu.VMEM((pair_w, pair_w), F32)],
        compiler_params=_params(("parallel", "parallel", "arbitrary")),
        name="moba_attn",
    )(proj, proj, proj, bias)


def _sb_kernel(q_ref, k_ref, v_ref, o_ref, kmax_ref):
    i = pl.program_id(1)
    blk = SB_BLOCK
    hd = SB_HEAD_DIM
    s_len = k_ref.shape[1]
    scale = hd ** -0.5

    @pl.when(i == 0)
    def _():
        chunk = 512
        for h in range(SB_HEADS):
            def norm_max(c, mx):
                kk = k_ref[0, pl.ds(pl.multiple_of(c * chunk, chunk), chunk), h * hd:(h + 1) * hd].astype(F32)
                return jnp.maximum(mx, (kk * kk).sum(-1, keepdims=True).max(0, keepdims=True))
            mx = lax.fori_loop(0, s_len // chunk, norm_max, jnp.zeros((1, 1), F32))
            kmax_ref[h] = jnp.broadcast_to(jnp.sqrt(mx), kmax_ref.shape[1:])

    row = lax.broadcasted_iota(jnp.int32, (blk, blk), 0)
    colb = lax.broadcasted_iota(jnp.int32, (blk, blk), 1)
    strictly_past = colb < row
    r2 = lax.broadcasted_iota(jnp.int32, (blk, 2 * blk), 0)
    c2 = lax.broadcasted_iota(jnp.int32, (blk, 2 * blk), 1)
    tri_ones = jnp.where((c2 >= blk) | (r2 >= c2), 1.0, 0.0).astype(BF16)
    qs = [q_ref[0, :, h * hd:(h + 1) * hd] for h in range(SB_HEADS)]
    z_bound = []
    for h in range(SB_HEADS):
        qf = qs[h].astype(F32)
        z_bound.append(jnp.sqrt((qf * qf).sum(-1, keepdims=True)) * kmax_ref[h][0:1, 0:1] * scale)

    def block(h, j, carry, acc, diagonal, exists=None):
        rows = pl.ds(pl.multiple_of(j * blk, blk), blk)
        kj = k_ref[0, rows, h * hd:(h + 1) * hd]
        vj = v_ref[0, rows, h * hd:(h + 1) * hd]
        z = _dot_nt(qs[h], kj) * scale
        lk = -(jnp.maximum(z, 0.0) + jnp.log1p(jnp.exp(-jnp.abs(z))))
        if diagonal:
            lk = jnp.where(strictly_past, lk, 0.0)
        if exists is not None:
            lk = jnp.where(exists, lk, 0.0)
        sums = _dot(lk.astype(BF16), tri_ones)
        log_a = sums[:, :blk] + carry + z
        if diagonal:
            log_a = jnp.where(strictly_past, log_a, NEG_INF)
        if exists is not None:
            log_a = jnp.where(exists, log_a, NEG_INF)
        acc = acc + _dot(jnp.exp(log_a).astype(BF16), vj)
        return carry + sums[:, blk:], acc

    def all_dead(carries):
        worst = None
        for h in range(SB_HEADS):
            w = (carries[h] * 0.98 + z_bound[h] * 1.02).max()
            worst = w if worst is None else jnp.maximum(worst, w)
        return worst < F32_EXP_ZERO_BELOW - 6.0

    state = [block(h, i, jnp.zeros((blk, blk), F32), jnp.zeros((blk, hd), F32), True) for h in range(SB_HEADS)]
    for t in range(SB_UNROLLED_BLOCKS):
        exists = i > t
        state = [block(h, jnp.maximum(i - 1 - t, 0), state[h][0], state[h][1], False, exists)
                 for h in range(SB_HEADS)]
    carries = tuple(st[0] for st in state)
    accs = tuple(st[1] for st in state)

    def cond(c):
        t, dead, _, _ = c
        return (t < i) & jnp.logical_not(dead)

    def body(c):
        t, _, carries, accs = c
        new = [block(h, i - 1 - t, carries[h], accs[h], False) for h in range(SB_HEADS)]
        carries = tuple(st[0] for st in new)
        return t + 1, all_dead(carries), carries, tuple(st[1] for st in new)

    _, _, _, accs = lax.while_loop(cond, body, (jnp.int32(SB_UNROLLED_BLOCKS), all_dead(carries), carries, accs))
    for h in range(SB_HEADS):
        o_ref[0, :, h * hd:(h + 1) * hd] = accs[h].astype(o_ref.dtype)


def _sb_attention(proj):
    b, s, pw = proj.shape
    nq = s // SB_BLOCK
    w = BRANCH_WIDTH
    return pl.pallas_call(
        _sb_kernel,
        grid=(b, nq),
        in_specs=[pl.BlockSpec((1, SB_BLOCK, w), lambda bi, i: (bi, i, QC)),
                  pl.BlockSpec((1, s, w), lambda bi, i: (bi, 0, KC)),
                  pl.BlockSpec((1, s, w), lambda bi, i: (bi, 0, VC))],
        out_specs=pl.BlockSpec((1, SB_BLOCK, w), lambda bi, i: (bi, i, 0)),
        out_shape=jax.ShapeDtypeStruct((b, s, w), BF16),
        scratch_shapes=[pltpu.VMEM((SB_HEADS, 8, 128), F32)],
        compiler_params=_params(("parallel", "arbitrary")),
        name="stick_breaking_attn",
    )(proj, proj, proj)


def _merge_kernel(x_ref, oa_ref, ob_ref, oc_ref, od_ref, g0, g1, g2, g3, w0, w1, w2, w3, out_ref):
    x = x_ref[...]
    acc = None
    for o_ref, wg_ref, wo_ref in ((oa_ref, g0, w0), (ob_ref, g1, w1), (oc_ref, g2, w2), (od_ref, g3, w3)):
        gate = jax.nn.sigmoid(_dot(x, wg_ref[...]))
        term = gate * _dot(o_ref[...], wo_ref[...])
        acc = term if acc is None else acc + term
    out_ref[...] = acc.astype(out_ref.dtype)


def _merge(x_bf, branches, w_gates, w_o, layer, tm=512, tn=512):
    n, d = x_bf.shape
    nj = d // tn
    row = lambda j, i: (i, 0)
    in_specs = [pl.BlockSpec((tm, d), row)] + [pl.BlockSpec((tm, BRANCH_WIDTH), row)] * N_BRANCHES
    in_specs += [pl.BlockSpec((d, tn), functools.partial(lambda j, i, br: (layer, br * nj + j), br=br))
                 for br in range(N_BRANCHES)]
    in_specs += [pl.BlockSpec((BRANCH_WIDTH, tn),
                              functools.partial(lambda j, i, br: (layer * N_BRANCHES + br, j), br=br))
                 for br in range(N_BRANCHES)]
    return pl.pallas_call(
        _merge_kernel,
        grid=(nj, n // tm),
        in_specs=in_specs,
        out_specs=pl.BlockSpec((tm, tn), lambda j, i: (i, j)),
        out_shape=jax.ShapeDtypeStruct((n, d), BF16),
        compiler_params=_params(("parallel", "arbitrary")),
        name="gated_merge",
    )(x_bf, *branches, *([w_gates] * N_BRANCHES), *([w_o] * N_BRANCHES))


def _layer_norm(r, g, b):
    mu = r.mean(-1, keepdims=True)
    var = jnp.square(r - mu).mean(-1, keepdims=True)
    return (r - mu) * lax.rsqrt(var + LN_EPS) * g + b


def _out_ln_kernel(m_ref, w_ref, x_ref, g_ref, b_ref, wrt_ref, xo_ref, xb_ref, lgt_ref):
    y = _dot(m_ref[...], w_ref[...])
    xn = _layer_norm(ALPHA * x_ref[...] + y, g_ref[...], b_ref[...])
    xo_ref[...] = xn
    xb = xn.astype(BF16)
    xb_ref[...] = xb
    lgt_ref[...] = _dot_nt(wrt_ref[...], xb)


def _out_ln(merged, w_out, x, g, b, w_router_t, layer, tm=512):
    n, d = x.shape
    e = N_EXPERTS
    row = lambda i: (i, 0)
    fix = lambda i: (0, 0)
    of_layer = lambda i: (layer, 0)
    return pl.pallas_call(
        _out_ln_kernel,
        grid=(n // tm,),
        in_specs=[pl.BlockSpec((tm, d), row), pl.BlockSpec((d, d), of_layer), pl.BlockSpec((tm, d), row),
                  pl.BlockSpec((1, d), fix), pl.BlockSpec((1, d), fix), pl.BlockSpec((e, d), of_layer)],
        out_specs=(pl.BlockSpec((tm, d), row), pl.BlockSpec((tm, d), row), pl.BlockSpec((e, tm), lambda i: (0, i))),
        out_shape=(jax.ShapeDtypeStruct((n, d), F32), jax.ShapeDtypeStruct((n, d), BF16),
                   jax.ShapeDtypeStruct((e, n), F32)),
        compiler_params=_params(("parallel",)),
        name="out_proj_ln_router",
    )(merged, w_out, x, g, b, w_router_t)


def _route_kernel(lg_ref, rb_ref, su_ref, dest_ref, gw_ref, cnt_ref, counts_ref, run_ref, pstart_ref, *, tb):
    p = pl.program_id(0)
    i = pl.program_id(1)
    ne, tm = lg_ref.shape
    gsz = ne // N_EXPERT_GROUPS
    scores = jax.nn.sigmoid(lg_ref[...])
    biased = scores + rb_ref[...]
    io_in = lax.broadcasted_iota(jnp.int32, (gsz, tm), 0)
    groups = [biased[g * gsz:(g + 1) * gsz, :] for g in range(N_EXPERT_GROUPS)]
    gscore = []
    for rows in groups:
        m1 = rows.max(0, keepdims=True)
        i1 = jnp.where(rows == m1, io_in, gsz).min(0, keepdims=True)
        m2 = jnp.where(io_in == i1, -jnp.inf, rows).max(0, keepdims=True)
        gscore.append(m1 + m2)
    masked = []
    for g in range(N_EXPERT_GROUPS):
        beaten = jnp.zeros((1, tm), F32)
        for o in range(N_EXPERT_GROUPS):
            if o == g:
                continue
            wins = (gscore[o] >= gscore[g]) if o < g else (gscore[o] > gscore[g])
            beaten = beaten + jnp.where(wins, 1.0, 0.0)
        masked.append(jnp.where(beaten < TOPK_GROUPS, groups[g], NEG_INF))
    cur = jnp.concatenate(masked, axis=0)
    io_e = lax.broadcasted_iota(jnp.int32, (ne, tm), 0)
    sel = jnp.zeros((ne, tm), F32)
    for _ in range(TOP_K):
        mx = cur.max(0, keepdims=True)
        first = jnp.where(cur == mx, io_e, ne).min(0, keepdims=True)
        pick = io_e == first
        sel = jnp.where(pick, 1.0, sel)
        cur = jnp.where(pick, -jnp.inf, cur)
    per_expert = sel.sum(1, keepdims=True)

    @pl.when((p == 0) & (i == 0))
    def _():
        counts_ref[...] = jnp.zeros_like(counts_ref)

    @pl.when(p == 0)
    def _():
        counts_ref[...] += per_expert
        dest_ref[...] = jnp.zeros_like(dest_ref)
        gw_ref[...] = jnp.zeros_like(gw_ref)
        cnt_ref[...] = jnp.zeros_like(cnt_ref)

    @pl.when((p == 1) & (i == 0))
    def _():
        padded = jnp.floor((counts_ref[...] + (tb - 1)) / tb) * tb
        r = lax.broadcasted_iota(jnp.int32, (ne, ne), 0)
        c = lax.broadcasted_iota(jnp.int32, (ne, ne), 1)
        padded_l = jnp.where(r == c, padded, 0.0).sum(0, keepdims=True)
        pstart_ref[...] = jnp.where(c < r, padded_l, 0.0).sum(1, keepdims=True)
        run_ref[...] = jnp.zeros_like(run_ref)

    @pl.when(p == 1)
    def _():
        sel_bf = sel.astype(BF16)
        slot = pstart_ref[...] + run_ref[...] + _dot(sel_bf, su_ref[...])
        run_ref[...] += per_expert
        r = lax.broadcasted_iota(jnp.int32, (ne, ne), 0)
        c = lax.broadcasted_iota(jnp.int32, (ne, ne), 1)
        order = _dot(jnp.where(c < r, 1.0, 0.0).astype(BF16), sel_bf)
        picked = sel * scores
        gwm = picked / picked.sum(0, keepdims=True) * ROUTED_SCALE
        for k in range(TOP_K):
            pk = (sel > 0.0) & (order == k)
            dest_ref[k:k + 1, :] = jnp.where(pk, slot, 0.0).sum(0, keepdims=True).astype(jnp.int32)
            gw_ref[k:k + 1, :] = jnp.where(pk, gwm, 0.0).sum(0, keepdims=True)
        cnt_ref[...] = jnp.broadcast_to(counts_ref[...], cnt_ref.shape)


def _route(logits_t, router_bias, tb, tm=512):
    ne, n = logits_t.shape
    strictly_earlier = jnp.asarray(np.triu(np.ones((tm, tm), np.float32), 1), BF16)
    fix = lambda p, i: (0, 0)
    out_blk = lambda p, i: (0, i * p)
    dest, gw, cnt = pl.pallas_call(
        functools.partial(_route_kernel, tb=tb),
        grid=(2, n // tm),
        in_specs=[pl.BlockSpec((ne, tm), lambda p, i: (0, i)), pl.BlockSpec((ne, 1), fix),
                  pl.BlockSpec((tm, tm), fix)],
        out_specs=(pl.BlockSpec((TOP_K, tm), out_blk), pl.BlockSpec((TOP_K, tm), out_blk),
                   pl.BlockSpec((ne, 128), fix)),
        out_shape=(jax.ShapeDtypeStruct((TOP_K, n), jnp.int32), jax.ShapeDtypeStruct((TOP_K, n), F32),
                   jax.ShapeDtypeStruct((ne, 128), F32)),
        scratch_shapes=[pltpu.VMEM((ne, 1), F32)] * 3,
        compiler_params=_params(("arbitrary", "arbitrary")),
        name="router_slots",
    )(logits_t, router_bias.astype(F32)[:, None], strictly_earlier)
    return dest, gw, cnt[:, 0]


def _expert_kernel(be_ref, used_ref, xs_ref, wg_ref, wu_ref, wd_ref, y_ref, wg_bf, wu_bf, wd_bf):
    i = pl.program_id(0)

    @pl.when((i == 0) | (be_ref[i] != be_ref[jnp.maximum(i - 1, 0)]))
    def _():
        wg_bf[...] = wg_ref[0].astype(BF16)
        wu_bf[...] = wu_ref[0].astype(BF16)
        wd_bf[...] = wd_ref[0].astype(BF16)

    @pl.when(used_ref[i] > 0)
    def _():
        xs = xs_ref[...]
        h = jax.nn.silu(_dot(xs, wg_bf[...])) * _dot(xs, wu_bf[...])
        y_ref[...] = _dot(h.astype(BF16), wd_bf[...]).astype(y_ref.dtype)


def _experts(xs, blk_expert, blk_used, wg, wu, wd, tb):
    ns, d = xs.shape
    hdn = wg.shape[2]
    grid_spec = pltpu.PrefetchScalarGridSpec(
        num_scalar_prefetch=2, grid=(ns // tb,),
        in_specs=[pl.BlockSpec((tb, d), lambda i, be, us: (i, 0)),
                  pl.BlockSpec((1, d, hdn), lambda i, be, us: (be[i], 0, 0)),
                  pl.BlockSpec((1, d, hdn), lambda i, be, us: (be[i], 0, 0)),
                  pl.BlockSpec((1, hdn, d), lambda i, be, us: (be[i], 0, 0))],
        out_specs=pl.BlockSpec((tb, d), lambda i, be, us: (i, 0)),
        scratch_shapes=[pltpu.VMEM((d, hdn), BF16), pltpu.VMEM((d, hdn), BF16), pltpu.VMEM((hdn, d), BF16)])
    return pl.pallas_call(
        _expert_kernel, grid_spec=grid_spec,
        out_shape=jax.ShapeDtypeStruct((ns, d), BF16),
        compiler_params=_params(("arbitrary",)),
        name="grouped_swiglu",
    )(blk_expert, blk_used, xs, wg, wu, wd)


def _final_ln_kernel(x_ref, sh_ref, ys_ref, gw_ref, g_ref, b_ref, xo_ref, xb_ref):
    ffn = sh_ref[...].astype(F32)
    gw = gw_ref[...]
    for kk in range(TOP_K):
        ffn = ffn + gw[:, kk:kk + 1] * ys_ref[kk].astype(F32)
    xn = _layer_norm(ALPHA * x_ref[...] + ffn, g_ref[...], b_ref[...])
    xo_ref[...] = xn
    xb_ref[...] = xn.astype(BF16)


def _final_ln(x, shared, ys_k, gw, g, b, tm=256):
    n, d = x.shape
    row = lambda i: (i, 0)
    fix = lambda i: (0, 0)
    return pl.pallas_call(
        _final_ln_kernel,
        grid=(n // tm,),
        in_specs=[pl.BlockSpec((tm, d), row), pl.BlockSpec((tm, d), row),
                  pl.BlockSpec((TOP_K, tm, d), lambda i: (0, i, 0)), pl.BlockSpec((tm, TOP_K), row),
                  pl.BlockSpec((1, d), fix), pl.BlockSpec((1, d), fix)],
        out_specs=(pl.BlockSpec((tm, d), row), pl.BlockSpec((tm, d), row)),
        out_shape=(jax.ShapeDtypeStruct((n, d), F32), jax.ShapeDtypeStruct((n, d), BF16)),
        compiler_params=_params(("parallel",)),
        name="moe_combine_ln",
    )(x, shared, ys_k, gw, g, b)


def _layer(layer, x, x_bf, biases, w_qkv, w_gates, w_o, w_out, w_router_t, sinks, ln1_g, ln1_b, router_bias,
           w_gate, w_up, w_down, w_sh_gate, w_sh_up, w_sh_down, ln2_g, ln2_b, b, s):
    n, d = x.shape
    bias_a, bias_b, bias_d = biases
    bw = BRANCH_WIDTH
    tm = min(1024, n)
    proj_a = _matmul(x_bf, w_qkv, layer, 0, PROJ_A_WIDTH // bw, BF16, tm, bw).reshape(b, s, PROJ_A_WIDTH)
    proj = _matmul(x_bf, w_qkv, layer, PROJ_A_WIDTH // bw, PROJ_R_WIDTH // bw, BF16, tm, bw).reshape(b, s, PROJ_R_WIDTH)

    parts = []
    for dil, ba in zip(A_DILATIONS, bias_a):
        o, lse = _band_attention(proj_a, QA, KA, VA, bw, ba, dil)
        parts += [o.reshape(n, bw), lse.reshape(n, bw)]
    o_a = _mix_combine(parts)
    o_b = _moba_attention(proj, bias_b).reshape(n, bw)
    o_c = _sb_attention(proj).reshape(n, bw)
    sink_row = jnp.repeat(sinks.astype(F32), HEAD_DIM)[None, :]
    o_d = _band_attention(proj, QD, KD_COL, VD_COL, KV_W, bias_d, 1, sink_row).reshape(n, bw)

    merged = _merge(x_bf, (o_a, o_b, o_c, o_d), w_gates, w_o, layer)
    x1, x1_bf, logits_t = _out_ln(merged, w_out, x, ln1_g[None, :], ln1_b[None, :], w_router_t, layer)

    tb = EXPERT_BLOCK
    dest, gw, counts = _route(logits_t, router_bias, tb)
    nblk = -(-n * TOP_K // tb) + N_EXPERTS
    pends = jnp.cumsum(jnp.ceil(counts / tb) * tb).astype(jnp.int32)
    blk_start = jnp.arange(nblk, dtype=jnp.int32) * tb
    blk_expert = jnp.minimum((pends[None, :] <= blk_start[:, None]).sum(1), N_EXPERTS - 1).astype(jnp.int32)
    blk_used = (blk_start < pends[-1]).astype(jnp.int32)
    token = jnp.broadcast_to(jnp.arange(n, dtype=jnp.int32)[None, :], dest.shape)
    slot_token = (jnp.arange(nblk * tb, dtype=jnp.int32) % n).at[dest.reshape(-1)].set(
        token.reshape(-1), unique_indices=True)
    xs = x1_bf[slot_token]
    ys = _experts(xs, blk_expert + layer * N_EXPERTS, blk_used, w_gate, w_up, w_down, tb)
    nsh = n // tb
    shared = _experts(x1_bf, jnp.full((nsh,), layer, jnp.int32), jnp.ones((nsh,), jnp.int32),
                      w_sh_gate, w_sh_up, w_sh_down, tb)
    return _final_ln(x1, shared, ys[dest], gw.T, ln2_g[None, :], ln2_b[None, :])


def kernel(x, rel_bias, w_in, w_o, w_out, attn_sinks, ln1_g, ln1_b, w_router, router_bias,
           w_gate, w_up, w_down, w_sh_gate, w_sh_up, w_sh_down, ln2_g, ln2_b):
    b, s, d = x.shape
    n = b * s
    h = MIX_HEADS
    nl, ne = w_gate.shape[:2]
    hdn = w_gate.shape[3]
    bias_a = [_band_bias(rel_bias[:, :h], win // dil, dil) for win, dil in zip(A_WINDOWS, A_DILATIONS)]
    bias_b = _moba_bias(rel_bias[:, h:2 * h])
    bias_d = _band_bias(rel_bias[:, 2 * h:], SWA_WINDOW - 1, 1)
    w_qkv, w_gates = _cast_w_in(w_in)
    w_o_bf = w_o.astype(BF16).reshape(nl * d, d)
    w_out_bf = w_out.astype(BF16).reshape(nl * d, d)
    w_router_t = w_router.transpose(0, 2, 1).astype(BF16).reshape(nl * ne, d)
    w_gate_s, w_up_s = w_gate.reshape(nl * ne, d, hdn), w_up.reshape(nl * ne, d, hdn)
    w_down_s = w_down.reshape(nl * ne, hdn, d)
    xf = x.reshape(n, d)
    xb = xf.astype(BF16)
    for l in range(nl):
        xf, xb = _layer(l, xf, xb, (bias_a, bias_b, bias_d), w_qkv, w_gates, w_o_bf, w_out_bf, w_router_t,
                        attn_sinks[l], ln1_g[l], ln1_b[l], router_bias[l], w_gate_s, w_up_s, w_down_s,
                        w_sh_gate, w_sh_up, w_sh_down, ln2_g[l], ln2_b[l], b, s)
    return xf.reshape(b, s, d)
```

```python
import functools
import math

import numpy as np
import jax
import jax.numpy as jnp
from jax import lax
from jax.experimental import pallas as pl
from jax.experimental.pallas import tpu as pltpu

D_MODEL = 2048
DEPTH = 4
HEAD_DIM = 64
N_BRANCHES = 4
BRANCH_WIDTH = D_MODEL // N_BRANCHES
MIX_HEADS = BRANCH_WIDTH // HEAD_DIM
SB_HEADS = 4
SB_HEAD_DIM = BRANCH_WIDTH // SB_HEADS
A_WINDOWS = (128, 512, 2048)
A_DILATIONS = (1, 4, 16)
MOBA_BLOCK = 256
MOBA_TOPK = 3
SB_BLOCK = 128
SB_UNROLLED_BLOCKS = 2
SWA_WINDOW = 128
D_KV_HEADS = 2
BAND_BLOCK = 128
BAND_SUBBLOCKS = 2
REL_BUCKETS = 32
REL_MAX_EXACT = 16
REL_MAX_DIST = 2048
N_EXPERTS = 64
TOP_K = 8
N_EXPERT_GROUPS = 8
TOPK_GROUPS = 4
EXPERT_HIDDEN = D_MODEL * 3 // 16
ROUTED_SCALE = 2.5
EXPERT_BLOCK = 512
MOE_TOKEN_GROUPS = 2
LN_EPS = 1e-5
ALPHA = (2 * DEPTH) ** 0.25
NEG_INF = -1e30
KV_W = D_KV_HEADS * HEAD_DIM
QKV_WIDTH = 10 * BRANCH_WIDTH + 2 * KV_W
MOBA_BIAS_TILES = 8
QA, KA, VA = range(3)
PROJ_A_WIDTH = 3 * BRANCH_WIDTH
QB, KB, VB, QC, KC, VC, QD = range(7)
PROJ_R_WIDTH = 8 * BRANCH_WIDTH
KD_COL = 7 * BRANCH_WIDTH // KV_W
VD_COL = KD_COL + 1
F32_EXP_ZERO_BELOW = -104.0

VMEM_LIMIT_BYTES = 56 * 1024 * 1024

F32 = jnp.float32
BF16 = jnp.bfloat16


def _params(semantics, vmem=VMEM_LIMIT_BYTES):
    return pltpu.CompilerParams(dimension_semantics=semantics, vmem_limit_bytes=vmem)


def _dot(a, b):
    return jnp.dot(a, b, preferred_element_type=F32)


def _dot_nt(a, b):
    return lax.dot_general(a, b, (((1,), (1,)), ((), ())), preferred_element_type=F32)


def _mm_kernel(x_ref, w_ref, o_ref):
    o_ref[...] = _dot(x_ref[...], w_ref[...]).astype(o_ref.dtype)


def _matmul(x, w, layer, col0, ncols, out_dtype, tm, tn):
    m, k = x.shape
    return pl.pallas_call(
        _mm_kernel,
        grid=(m // tm, ncols),
        in_specs=[pl.BlockSpec((tm, k), lambda i, j: (i, 0)),
                  pl.BlockSpec((k, tn), lambda i, j: (layer, col0 + j))],
        out_specs=pl.BlockSpec((tm, tn), lambda i, j: (i, j)),
        out_shape=jax.ShapeDtypeStruct((m, ncols * tn), out_dtype),
        compiler_params=_params(("parallel", "arbitrary")),
        name="qkv_proj",
    )(x, w)


def _cast_split_kernel(w_ref, qkv_ref, gates_ref):
    w = w_ref[...]
    qkv_ref[...] = w[:, :qkv_ref.shape[1]].astype(BF16)
    gates_ref[...] = w[:, QKV_WIDTH:].astype(BF16)


def _cast_w_in(w_in, rows=128):
    nl, d, in_w = w_in.shape
    lead = PROJ_A_WIDTH + PROJ_R_WIDTH
    return pl.pallas_call(
        _cast_split_kernel,
        grid=(nl * d // rows,),
        in_specs=[pl.BlockSpec((rows, in_w), lambda i: (i, 0))],
        out_specs=(pl.BlockSpec((rows, lead), lambda i: (i, 0)),
                   pl.BlockSpec((rows, in_w - QKV_WIDTH), lambda i: (i, 0))),
        out_shape=(jax.ShapeDtypeStruct((nl * d, lead), BF16),
                   jax.ShapeDtypeStruct((nl * d, in_w - QKV_WIDTH), BF16)),
        compiler_params=_params(("parallel",)),
        name="cast_w_in",
    )(w_in.reshape(nl * d, in_w))


def _t5_bucket(dist):
    n = np.maximum(dist, 0)
    nf = np.maximum(n, 1).astype(np.float32)
    ratio = np.log(nf / np.float32(REL_MAX_EXACT)) / np.float32(math.log(REL_MAX_DIST / REL_MAX_EXACT))
    large = REL_MAX_EXACT + (ratio * np.float32(REL_BUCKETS - REL_MAX_EXACT)).astype(np.int32)
    return np.where(n < REL_MAX_EXACT, n, np.minimum(large, REL_BUCKETS - 1)).astype(np.int32)


def _bias_lookup(tab, bucket, keep):
    bk = jnp.asarray(np.where(keep, bucket, -1), jnp.int32)[None]
    tab_t = tab.astype(F32).T
    out = jnp.full((tab.shape[1],) + bucket.shape, NEG_INF, F32)
    for bkt in range(REL_BUCKETS):
        out = jnp.where(bk == bkt, tab_t[:, bkt].reshape((-1,) + (1,) * bucket.ndim), out)
    return out


def _band_bias(tab, max_steps, dist_scale):
    qi = np.arange(BAND_BLOCK)[:, None]
    kj = np.arange(2 * BAND_BLOCK)[None, :]
    dist = BAND_BLOCK + qi - kj
    return _bias_lookup(tab, _t5_bucket(dist * dist_scale), (dist >= 0) & (dist <= max_steps))


def _moba_bias(tab):
    r = np.arange(MOBA_BLOCK)[:, None]
    c = np.arange(MOBA_BLOCK)[None, :]
    dist = np.stack([dl * MOBA_BLOCK + r - c for dl in range(MOBA_BIAS_TILES)])
    return _bias_lookup(tab, _t5_bucket(dist), dist >= 0)


def _duplicate_kv_heads(t):
    assert t.shape[1] == 2 * HEAD_DIM
    tf = t.astype(F32)
    swapped = pltpu.roll(tf, HEAD_DIM, axis=1)
    first = lax.broadcasted_iota(jnp.int32, (1, 2 * HEAD_DIM), 1) < HEAD_DIM
    return jnp.where(first, tf, swapped).astype(t.dtype), jnp.where(first, swapped, tf).astype(t.dtype)


def _band_kernel(*refs, kv_rep, with_sink):
    if with_sink:
        q_ref, kp_ref, ko_ref, vp_ref, vo_ref, bias_ref, sink_ref, o_ref = refs
    else:
        q_ref, kp_ref, ko_ref, vp_ref, vo_ref, bias_ref, o_ref, lse_ref = refs
    blk = BAND_BLOCK
    pair_w = 2 * HEAD_DIM
    first = lax.broadcasted_iota(jnp.int32, (1, pair_w), 1) < HEAD_DIM
    own_cols = lax.broadcasted_iota(jnp.int32, (1, 2 * blk), 1) >= blk
    k_all = jnp.concatenate([kp_ref[0], ko_ref[0]], axis=0)
    v_all = jnp.concatenate([vp_ref[0], vo_ref[0]], axis=0)
    if kv_rep > 1:
        k_dup, v_dup = _duplicate_kv_heads(k_all), _duplicate_kv_heads(v_all)
    for sb in range(BAND_SUBBLOCKS):
        rows_q = slice(sb * blk, (sb + 1) * blk)
        keys = slice(sb * blk, (sb + 2) * blk)
        q = q_ref[0, rows_q, :] * (HEAD_DIM ** -0.5)
        outs, lses = [], []
        for p2 in range(MIX_HEADS // 2):
            cols = slice(p2 * pair_w, (p2 + 1) * pair_w)
            qp = q[:, cols]
            zero = jnp.zeros_like(qp)
            q2 = jnp.concatenate([jnp.where(first, qp, zero), jnp.where(first, zero, qp)], axis=0)
            if kv_rep > 1:
                g = 2 * p2 // kv_rep
                k, v = k_dup[g][keys], v_dup[g][keys]
            else:
                k, v = k_all[keys, cols], v_all[keys, cols]
            s = _dot_nt(q2, k) + bias_ref[2 * p2:2 * p2 + 2].reshape(2 * blk, 2 * blk)
            if sb == 0:
                s = jnp.where(jnp.logical_or(own_cols, pl.program_id(2) > 0), s, NEG_INF)
            m = s.max(-1, keepdims=True)
            p = jnp.exp(s - m)
            den = p.sum(-1, keepdims=True)
            o2 = _dot(p.astype(BF16), v) / den
            lse2 = m + jnp.log(den)
            outs.append(jnp.where(first, o2[:blk], o2[blk:]))
            lses.append(jnp.where(first, lse2[:blk], lse2[blk:]))
        o = jnp.concatenate(outs, axis=1)
        lse = jnp.concatenate(lses, axis=1)
        if with_sink:
            o_ref[0, rows_q, :] = (o * jax.nn.sigmoid(lse - sink_ref[...])).astype(o_ref.dtype)
        else:
            o_ref[0, rows_q, :] = o.astype(o_ref.dtype)
            lse_ref[0, rows_q, :] = lse


def _band_attention(proj, qcol, kcol, vcol, kvw, bias, dil, sink_row=None):
    b, s, pw = proj.shape
    qw = BRANCH_WIDTH
    l = s // dil
    rows = BAND_SUBBLOCKS * BAND_BLOCK
    nb = l // rows
    kv_rep = qw // kvw
    pv = proj.reshape(b, l, dil * pw)
    qpr, kpr = pw // qw, pw // kvw
    prev_blk = lambda n: jnp.maximum(BAND_SUBBLOCKS * n - 1, 0)
    own = lambda bi, r, n: (bi, n, r)
    q_own = lambda bi, r, n: (bi, n, r * qpr + qcol)
    k_own = lambda bi, r, n: (bi, n, r * kpr + kcol)
    k_prev = lambda bi, r, n: (bi, prev_blk(n), r * kpr + kcol)
    v_own = lambda bi, r, n: (bi, n, r * kpr + vcol)
    v_prev = lambda bi, r, n: (bi, prev_blk(n), r * kpr + vcol)
    in_specs = [pl.BlockSpec((1, rows, qw), q_own),
                pl.BlockSpec((1, BAND_BLOCK, kvw), k_prev),
                pl.BlockSpec((1, rows, kvw), k_own),
                pl.BlockSpec((1, BAND_BLOCK, kvw), v_prev),
                pl.BlockSpec((1, rows, kvw), v_own),
                pl.BlockSpec((MIX_HEADS, BAND_BLOCK, 2 * BAND_BLOCK), lambda bi, r, n: (0, 0, 0))]
    args = [pv, pv, pv, pv, pv, bias]
    with_sink = sink_row is not None
    if with_sink:
        in_specs.append(pl.BlockSpec((1, qw), lambda bi, r, n: (0, 0)))
        args.append(sink_row)
        out_shape = jax.ShapeDtypeStruct((b, l, dil * qw), BF16)
        out_specs = pl.BlockSpec((1, rows, qw), own)
    else:
        out_shape = (jax.ShapeDtypeStruct((b, l, dil * qw), F32),) * 2
        out_specs = (pl.BlockSpec((1, rows, qw), own),) * 2
    out = pl.pallas_call(
        functools.partial(_band_kernel, kv_rep=kv_rep, with_sink=with_sink),
        grid=(b, dil, nb),
        in_specs=in_specs, out_specs=out_specs, out_shape=out_shape,
        compiler_params=_params(("parallel", "parallel", "arbitrary")),
        name="band_attn_sink" if with_sink else f"band_attn_d{dil}",
    )(*args)
    if with_sink:
        return out.reshape(b, s, qw)
    return out[0].reshape(b, s, qw), out[1].reshape(b, s, qw)


def _mix_combine_kernel(o1, l1, o2, l2, o3, l3, out_ref):
    a, bq, c = l1[...], l2[...], l3[...]
    m = jnp.maximum(jnp.maximum(a, bq), c)
    ea, eb, ec = jnp.exp(a - m), jnp.exp(bq - m), jnp.exp(c - m)
    den = ea + eb + ec
    out_ref[...] = (ea / den * o1[...] + eb / den * o2[...] + ec / den * o3[...]).astype(out_ref.dtype)


def _mix_combine(parts, tm=512):
    n, w = parts[0].shape
    spec = pl.BlockSpec((tm, w), lambda i: (i, 0))
    return pl.pallas_call(
        _mix_combine_kernel, grid=(n // tm,),
        in_specs=[spec] * 6, out_specs=spec,
        out_shape=jax.ShapeDtypeStruct((n, w), BF16),
        compiler_params=_params(("parallel",)),
        name="dilated_combine",
    )(*parts)


def _moba_kernel(q_ref, k_ref, v_ref, bias_ref, o_ref, kmean_ref, *, nbk):
    i = pl.program_id(2)
    blk = MOBA_BLOCK
    pair_w = 2 * HEAD_DIM

    nslot = kmean_ref.shape[0]
    assert nbk <= nslot

    @pl.when(i == 0)
    def _():
        kk = k_ref[0].astype(F32).reshape(nbk, blk, pair_w)
        kmean_ref[...] = jnp.zeros_like(kmean_ref)
        kmean_ref[0:nbk, :] = kk.sum(axis=1) * (1.0 / blk)

    q = q_ref[0]
    lane = lax.broadcasted_iota(jnp.int32, (1, pair_w), 1)
    zero = jnp.zeros_like(q)
    qs = jnp.concatenate([jnp.where(lane < HEAD_DIM, q, zero), jnp.where(lane >= HEAD_DIM, q, zero)], axis=0)
    qsc = qs * (HEAD_DIM ** -0.5)
    col = lax.broadcasted_iota(jnp.int32, (2 * blk, nslot), 1)
    gate = jnp.where(col < i, _dot_nt(qs, kmean_ref[...].astype(BF16)), NEG_INF)
    sel = jnp.zeros((2 * blk, nslot), F32)
    for _ in range(MOBA_TOPK):
        mx = gate.max(-1, keepdims=True)
        first = jnp.where(gate == mx, col, nslot).min(-1, keepdims=True)
        pick = col == first
        sel = jnp.where(pick & (col < i), 1.0, sel)
        gate = jnp.where(pick, -jnp.inf, gate)

    def bias_tile(dl):
        dl = jnp.clip(dl, 0, MOBA_BIAS_TILES - 1)
        return jnp.concatenate([bias_ref[0, dl], bias_ref[1, dl]], axis=0)

    own = pl.ds(pl.multiple_of(i * blk, blk), blk)
    s = _dot_nt(qsc, k_ref[0, own, :]) + bias_tile(0)
    m0 = s.max(-1, keepdims=True)
    p = jnp.exp(s - m0)
    l0 = p.sum(-1, keepdims=True)
    acc0 = _dot(p.astype(BF16), v_ref[0, own, :])
    q_aug = jnp.concatenate([qsc, jnp.where(sel > 0.0, 0.0, NEG_INF).astype(BF16)], axis=1)
    tag_of = (lax.broadcasted_iota(jnp.int32, (2 * blk, nslot), 1)
              - lax.broadcasted_iota(jnp.int32, (2 * blk, nslot), 0) // blk)

    def chunk_rows(c):
        return pl.ds(pl.multiple_of(2 * c * blk, 2 * blk), 2 * blk)

    def scores(c):
        ja = 2 * c
        key_tag = jnp.where(tag_of == ja, 1.0, 0.0).astype(BF16)
        bias = jnp.concatenate([bias_tile(i - ja), bias_tile(i - ja - 1)], axis=1)
        return _dot_nt(q_aug, jnp.concatenate([k_ref[0, chunk_rows(c), :], key_tag], axis=1)) + bias

    def body(c, carry):
        m, l, acc = carry
        s = scores(c)
        m_new = jnp.maximum(m, s.max(-1, keepdims=True))
        scale = jnp.exp(m - m_new)
        p = jnp.exp(s - m_new)
        l = scale * l + p.sum(-1, keepdims=True)
        acc = scale * acc + _dot(p.astype(BF16), v_ref[0, chunk_rows(c), :])
        return m_new, l, acc

    _, l, acc = lax.fori_loop(0, (i + 1) // 2, body, (m0, l0, acc0))
    o = acc / l
    o_ref[0] = jnp.where(lane < HEAD_DIM, o[:blk], o[blk:]).astype(o_ref.dtype)


def _moba_attention(proj, bias):
    b, s, pw = proj.shape
    nbk = s // MOBA_BLOCK
    pair_w = 2 * HEAD_DIM
    npair = BRANCH_WIDTH // pair_w
    slab = BRANCH_WIDTH // pair_w
    return pl.pallas_call(
        functools.partial(_moba_kernel, nbk=nbk),
        grid=(b, npair, nbk),
        in_specs=[pl.BlockSpec((1, MOBA_BLOCK, pair_w), lambda bi, p, i: (bi, i, QB * slab + p)),
                  pl.BlockSpec((1, s, pair_w), lambda bi, p, i: (bi, 0, KB * slab + p)),
                  pl.BlockSpec((1, s, pair_w), lambda bi, p, i: (bi, 0, VB * slab + p)),
                  pl.BlockSpec((2, MOBA_BIAS_TILES, MOBA_BLOCK, MOBA_BLOCK), lambda bi, p, i: (p, 0, 0, 0))],
        out_specs=pl.BlockSpec((1, MOBA_BLOCK, pair_w), lambda bi, p, i: (bi, i, p)),
        out_shape=jax.ShapeDtypeStruct((b, s, BRANCH_WIDTH), BF16),
        scratch_shapes=[pltpu.VMEM((pair_w, pair_w), F32)],
        compiler_params=_params(("parallel", "parallel", "arbitrary")),
        name="moba_attn",
    )(proj, proj, proj, bias)


def _sb_kernel(q_ref, k_ref, v_ref, o_ref, kmax_ref):
    i = pl.program_id(1)
    blk = SB_BLOCK
    hd = SB_HEAD_DIM
    s_len = k_ref.shape[1]
    scale = hd ** -0.5

    @pl.when(i == 0)
    def _():
        chunk = 512
        for h in range(SB_HEADS):
            def norm_max(c, mx):
                kk = k_ref[0, pl.ds(pl.multiple_of(c * chunk, chunk), chunk), h * hd:(h + 1) * hd].astype(F32)
                return jnp.maximum(mx, (kk * kk).sum(-1, keepdims=True).max(0, keepdims=True))
            mx = lax.fori_loop(0, s_len // chunk, norm_max, jnp.zeros((1, 1), F32))
            kmax_ref[h] = jnp.broadcast_to(jnp.sqrt(mx), kmax_ref.shape[1:])

    row = lax.broadcasted_iota(jnp.int32, (blk, blk), 0)
    colb = lax.broadcasted_iota(jnp.int32, (blk, blk), 1)
    strictly_past = colb < row
    r2 = lax.broadcasted_iota(jnp.int32, (blk, 2 * blk), 0)
    c2 = lax.broadcasted_iota(jnp.int32, (blk, 2 * blk), 1)
    tri_ones = jnp.where((c2 >= blk) | (r2 >= c2), 1.0, 0.0).astype(BF16)
    qs = [q_ref[0, :, h * hd:(h + 1) * hd] for h in range(SB_HEADS)]
    z_bound = []
    for h in range(SB_HEADS):
        qf = qs[h].astype(F32)
        z_bound.append(jnp.sqrt((qf * qf).sum(-1, keepdims=True)) * kmax_ref[h][0:1, 0:1] * scale)

    def block(h, j, carry, acc, diagonal, exists=None):
        rows = pl.ds(pl.multiple_of(j * blk, blk), blk)
        kj = k_ref[0, rows, h * hd:(h + 1) * hd]
        vj = v_ref[0, rows, h * hd:(h + 1) * hd]
        z = _dot_nt(qs[h], kj) * scale
        lk = -(jnp.maximum(z, 0.0) + jnp.log1p(jnp.exp(-jnp.abs(z))))
        if diagonal:
            lk = jnp.where(strictly_past, lk, 0.0)
        if exists is not None:
            lk = jnp.where(exists, lk, 0.0)
        sums = _dot(lk.astype(BF16), tri_ones)
        log_a = sums[:, :blk] + carry + z
        if diagonal:
            log_a = jnp.where(strictly_past, log_a, NEG_INF)
        if exists is not None:
            log_a = jnp.where(exists, log_a, NEG_INF)
        acc = acc + _dot(jnp.exp(log_a).astype(BF16), vj)
        return carry + sums[:, blk:], acc

    def all_dead(carries):
        worst = None
        for h in range(SB_HEADS):
            w = (carries[h] * 0.98 + z_bound[h] * 1.02).max()
            worst = w if worst is None else jnp.maximum(worst, w)
        return worst < F32_EXP_ZERO_BELOW - 6.0

    state = [block(h, i, jnp.zeros((blk, blk), F32), jnp.zeros((blk, hd), F32), True) for h in range(SB_HEADS)]
    for t in range(SB_UNROLLED_BLOCKS):
        exists = i > t
        state = [block(h, jnp.maximum(i - 1 - t, 0), state[h][0], state[h][1], False, exists)
                 for h in range(SB_HEADS)]
    carries = tuple(st[0] for st in state)
    accs = tuple(st[1] for st in state)

    def cond(c):
        t, dead, _, _ = c
        return (t < i) & jnp.logical_not(dead)

    def body(c):
        t, _, carries, accs = c
        new = [block(h, i - 1 - t, carries[h], accs[h], False) for h in range(SB_HEADS)]
        carries = tuple(st[0] for st in new)
        return t + 1, all_dead(carries), carries, tuple(st[1] for st in new)

    _, _, _, accs = lax.while_loop(cond, body, (jnp.int32(SB_UNROLLED_BLOCKS), all_dead(carries), carries, accs))
    for h in range(SB_HEADS):
        o_ref[0, :, h * hd:(h + 1) * hd] = accs[h].astype(o_ref.dtype)


def _sb_attention(proj):
    b, s, pw = proj.shape
    nq = s // SB_BLOCK
    w = BRANCH_WIDTH
    return pl.pallas_call(
        _sb_kernel,
        grid=(b, nq),
        in_specs=[pl.BlockSpec((1, SB_BLOCK, w), lambda bi, i: (bi, i, QC)),
                  pl.BlockSpec((1, s, w), lambda bi, i: (bi, 0, KC)),
                  pl.BlockSpec((1, s, w), lambda bi, i: (bi, 0, VC))],
        out_specs=pl.BlockSpec((1, SB_BLOCK, w), lambda bi, i: (bi, i, 0)),
        out_shape=jax.ShapeDtypeStruct((b, s, w), BF16),
        scratch_shapes=[pltpu.VMEM((SB_HEADS, 8, 128), F32)],
        compiler_params=_params(("parallel", "arbitrary")),
        name="stick_breaking_attn",
    )(proj, proj, proj)


def _merge_kernel(x_ref, oa_ref, ob_ref, oc_ref, od_ref, g0, g1, g2, g3, w0, w1, w2, w3, out_ref):
    x = x_ref[...]
    acc = None
    for o_ref, wg_ref, wo_ref in ((oa_ref, g0, w0), (ob_ref, g1, w1), (oc_ref, g2, w2), (od_ref, g3, w3)):
        gate = jax.nn.sigmoid(_dot(x, wg_ref[...]))
        term = gate * _dot(o_ref[...], wo_ref[...])
        acc = term if acc is None else acc + term
    out_ref[...] = acc.astype(out_ref.dtype)


def _merge(x_bf, branches, w_gates, w_o, layer, tm=512, tn=512):
    n, d = x_bf.shape
    nj = d // tn
    row = lambda j, i: (i, 0)
    in_specs = [pl.BlockSpec((tm, d), row)] + [pl.BlockSpec((tm, BRANCH_WIDTH), row)] * N_BRANCHES
    in_specs += [pl.BlockSpec((d, tn), functools.partial(lambda j, i, br: (layer, br * nj + j), br=br))
                 for br in range(N_BRANCHES)]
    in_specs += [pl.BlockSpec((BRANCH_WIDTH, tn),
                              functools.partial(lambda j, i, br: (layer * N_BRANCHES + br, j), br=br))
                 for br in range(N_BRANCHES)]
    return pl.pallas_call(
        _merge_kernel,
        grid=(nj, n // tm),
        in_specs=in_specs,
        out_specs=pl.BlockSpec((tm, tn), lambda j, i: (i, j)),
        out_shape=jax.ShapeDtypeStruct((n, d), BF16),
        compiler_params=_params(("parallel", "arbitrary")),
        name="gated_merge",
    )(x_bf, *branches, *([w_gates] * N_BRANCHES), *([w_o] * N_BRANCHES))


def _layer_norm(r, g, b):
    mu = r.mean(-1, keepdims=True)
    var = jnp.square(r - mu).mean(-1, keepdims=True)
    return (r - mu) * lax.rsqrt(var + LN_EPS) * g + b


def _out_ln_kernel(m_ref, w_ref, x_ref, g_ref, b_ref, wrt_ref, xo_ref, xb_ref, lgt_ref):
    y = _dot(m_ref[...], w_ref[...])
    xn = _layer_norm(ALPHA * x_ref[...] + y, g_ref[...], b_ref[...])
    xo_ref[...] = xn
    xb = xn.astype(BF16)
    xb_ref[...] = xb
    lgt_ref[...] = _dot_nt(wrt_ref[...], xb)


def _out_ln(merged, w_out, x, g, b, w_router_t, layer, tm=512):
    n, d = x.shape
    e = N_EXPERTS
    row = lambda i: (i, 0)
    fix = lambda i: (0, 0)
    of_layer = lambda i: (layer, 0)
    return pl.pallas_call(
        _out_ln_kernel,
        grid=(n // tm,),
        in_specs=[pl.BlockSpec((tm, d), row), pl.BlockSpec((d, d), of_layer), pl.BlockSpec((tm, d), row),
                  pl.BlockSpec((1, d), fix), pl.BlockSpec((1, d), fix), pl.BlockSpec((e, d), of_layer)],
        out_specs=(pl.BlockSpec((tm, d), row), pl.BlockSpec((tm, d), row), pl.BlockSpec((e, tm), lambda i: (0, i))),
        out_shape=(jax.ShapeDtypeStruct((n, d), F32), jax.ShapeDtypeStruct((n, d), BF16),
                   jax.ShapeDtypeStruct((e, n), F32)),
        compiler_params=_params(("parallel",)),
        name="out_proj_ln_router",
    )(merged, w_out, x, g, b, w_router_t)


def _route_kernel(lg_ref, rb_ref, su_ref, dest_ref, gw_ref, cnt_ref, counts_ref, run_ref, pstart_ref, *, tb):
    p = pl.program_id(0)
    i = pl.program_id(1)
    ne, tm = lg_ref.shape
    gsz = ne // N_EXPERT_GROUPS
    scores = jax.nn.sigmoid(lg_ref[...])
    biased = scores + rb_ref[...]
    io_in = lax.broadcasted_iota(jnp.int32, (gsz, tm), 0)
    groups = [biased[g * gsz:(g + 1) * gsz, :] for g in range(N_EXPERT_GROUPS)]
    gscore = []
    for rows in groups:
        m1 = rows.max(0, keepdims=True)
        i1 = jnp.where(rows == m1, io_in, gsz).min(0, keepdims=True)
        m2 = jnp.where(io_in == i1, -jnp.inf, rows).max(0, keepdims=True)
        gscore.append(m1 + m2)
    masked = []
    for g in range(N_EXPERT_GROUPS):
        beaten = jnp.zeros((1, tm), F32)
        for o in range(N_EXPERT_GROUPS):
            if o == g:
                continue
            wins = (gscore[o] >= gscore[g]) if o < g else (gscore[o] > gscore[g])
            beaten = beaten + jnp.where(wins, 1.0, 0.0)
        masked.append(jnp.where(beaten < TOPK_GROUPS, groups[g], NEG_INF))
    cur = jnp.concatenate(masked, axis=0)
    io_e = lax.broadcasted_iota(jnp.int32, (ne, tm), 0)
    sel = jnp.zeros((ne, tm), F32)
    for _ in range(TOP_K):
        mx = cur.max(0, keepdims=True)
        first = jnp.where(cur == mx, io_e, ne).min(0, keepdims=True)
        pick = io_e == first
        sel = jnp.where(pick, 1.0, sel)
        cur = jnp.where(pick, -jnp.inf, cur)
    per_expert = sel.sum(1, keepdims=True)

    @pl.when((p == 0) & (i == 0))
    def _():
        counts_ref[...] = jnp.zeros_like(counts_ref)

    @pl.when(p == 0)
    def _():
        counts_ref[...] += per_expert
        dest_ref[...] = jnp.zeros_like(dest_ref)
        gw_ref[...] = jnp.zeros_like(gw_ref)
        cnt_ref[...] = jnp.zeros_like(cnt_ref)

    @pl.when((p == 1) & (i == 0))
    def _():
        padded = jnp.floor((counts_ref[...] + (tb - 1)) / tb) * tb
        r = lax.broadcasted_iota(jnp.int32, (ne, ne), 0)
        c = lax.broadcasted_iota(jnp.int32, (ne, ne), 1)
        padded_l = jnp.where(r == c, padded, 0.0).sum(0, keepdims=True)
        pstart_ref[...] = jnp.where(c < r, padded_l, 0.0).sum(1, keepdims=True)
        run_ref[...] = jnp.zeros_like(run_ref)

    @pl.when(p == 1)
    def _():
        sel_bf = sel.astype(BF16)
        slot = pstart_ref[...] + run_ref[...] + _dot(sel_bf, su_ref[...])
        run_ref[...] += per_expert
        r = lax.broadcasted_iota(jnp.int32, (ne, ne), 0)
        c = lax.broadcasted_iota(jnp.int32, (ne, ne), 1)
        order = _dot(jnp.where(c < r, 1.0, 0.0).astype(BF16), sel_bf)
        picked = sel * scores
        gwm = picked / picked.sum(0, keepdims=True) * ROUTED_SCALE
        for k in range(TOP_K):
            pk = (sel > 0.0) & (order == k)
            dest_ref[k:k + 1, :] = jnp.where(pk, slot, 0.0).sum(0, keepdims=True).astype(jnp.int32)
            gw_ref[k:k + 1, :] = jnp.where(pk, gwm, 0.0).sum(0, keepdims=True)
        cnt_ref[...] = jnp.broadcast_to(counts_ref[...], cnt_ref.shape)


def _route(logits_t, router_bias, tb, first_token, n, tm=512):
    ne = logits_t.shape[0]
    blk0 = first_token // tm
    strictly_earlier = jnp.asarray(np.triu(np.ones((tm, tm), np.float32), 1), BF16)
    fix = lambda p, i: (0, 0)
    out_blk = lambda p, i: (0, i * p)
    dest, gw, cnt = pl.pallas_call(
        functools.partial(_route_kernel, tb=tb),
        grid=(2, n // tm),
        in_specs=[pl.BlockSpec((ne, tm), lambda p, i: (0, i + blk0)), pl.BlockSpec((ne, 1), fix),
                  pl.BlockSpec((tm, tm), fix)],
        out_specs=(pl.BlockSpec((TOP_K, tm), out_blk), pl.BlockSpec((TOP_K, tm), out_blk),
                   pl.BlockSpec((ne, 128), fix)),
        out_shape=(jax.ShapeDtypeStruct((TOP_K, n), jnp.int32), jax.ShapeDtypeStruct((TOP_K, n), F32),
                   jax.ShapeDtypeStruct((ne, 128), F32)),
        scratch_shapes=[pltpu.VMEM((ne, 1), F32)] * 3,
        compiler_params=_params(("arbitrary", "arbitrary")),
        name="router_slots",
    )(logits_t, router_bias.astype(F32)[:, None], strictly_earlier)
    return dest, gw, cnt[:, 0]


def _expert_kernel(be_ref, used_ref, xs_ref, wg_ref, wu_ref, wd_ref, y_ref, wg_bf, wu_bf, wd_bf):
    i = pl.program_id(0)

    @pl.when((i == 0) | (be_ref[i] != be_ref[jnp.maximum(i - 1, 0)]))
    def _():
        wg_bf[...] = wg_ref[0].astype(BF16)
        wu_bf[...] = wu_ref[0].astype(BF16)
        wd_bf[...] = wd_ref[0].astype(BF16)

    @pl.when(used_ref[i] > 0)
    def _():
        xs = xs_ref[...]
        h = jax.nn.silu(_dot(xs, wg_bf[...])) * _dot(xs, wu_bf[...])
        y_ref[...] = _dot(h.astype(BF16), wd_bf[...]).astype(y_ref.dtype)


def _experts(xs, blk_expert, blk_used, wg, wu, wd, tb):
    ns, d = xs.shape
    hdn = wg.shape[2]
    grid_spec = pltpu.PrefetchScalarGridSpec(
        num_scalar_prefetch=2, grid=(ns // tb,),
        in_specs=[pl.BlockSpec((tb, d), lambda i, be, us: (i, 0)),
                  pl.BlockSpec((1, d, hdn), lambda i, be, us: (be[i], 0, 0)),
                  pl.BlockSpec((1, d, hdn), lambda i, be, us: (be[i], 0, 0)),
                  pl.BlockSpec((1, hdn, d), lambda i, be, us: (be[i], 0, 0))],
        out_specs=pl.BlockSpec((tb, d), lambda i, be, us: (i, 0)),
        scratch_shapes=[pltpu.VMEM((d, hdn), BF16), pltpu.VMEM((d, hdn), BF16), pltpu.VMEM((hdn, d), BF16)])
    return pl.pallas_call(
        _expert_kernel, grid_spec=grid_spec,
        out_shape=jax.ShapeDtypeStruct((ns, d), BF16),
        compiler_params=_params(("arbitrary",)),
        name="grouped_swiglu",
    )(blk_expert, blk_used, xs, wg, wu, wd)


def _final_ln_kernel(x_ref, sh_ref, *rest, group_steps):
    ys_refs = rest[:-5]
    gw_ref, g_ref, b_ref, xo_ref, xb_ref = rest[-5:]
    group = pl.program_id(0) // group_steps

    def finish(ys_ref):
        ffn = sh_ref[...].astype(F32)
        gw = gw_ref[...]
        for kk in range(TOP_K):
            ffn = ffn + gw[:, kk:kk + 1] * ys_ref[kk].astype(F32)
        xn = _layer_norm(ALPHA * x_ref[...] + ffn, g_ref[...], b_ref[...])
        xo_ref[...] = xn
        xb_ref[...] = xn.astype(BF16)

    for gi, ys_ref in enumerate(ys_refs):
        pl.when(group == gi)(functools.partial(finish, ys_ref))


def _final_ln(x, shared, ys_groups, gw, g, b, tm=256):
    n, d = x.shape
    ngroups = len(ys_groups)
    group_steps = n // ngroups // tm
    row = lambda i: (i, 0)
    fix = lambda i: (0, 0)
    ys_specs = [pl.BlockSpec((TOP_K, tm, d), functools.partial(
        lambda i, gi: (0, jnp.clip(i - gi * group_steps, 0, group_steps - 1), 0), gi=gi)) for gi in range(ngroups)]
    return pl.pallas_call(
        functools.partial(_final_ln_kernel, group_steps=group_steps),
        grid=(n // tm,),
        in_specs=[pl.BlockSpec((tm, d), row), pl.BlockSpec((tm, d), row)] + ys_specs +
                 [pl.BlockSpec((tm, TOP_K), row), pl.BlockSpec((1, d), fix), pl.BlockSpec((1, d), fix)],
        out_specs=(pl.BlockSpec((tm, d), row), pl.BlockSpec((tm, d), row)),
        out_shape=(jax.ShapeDtypeStruct((n, d), F32), jax.ShapeDtypeStruct((n, d), BF16)),
        compiler_params=_params(("arbitrary",)),
        name="moe_combine_ln",
    )(x, shared, *ys_groups, gw, g, b)


def _layer(layer, x, x_bf, biases, w_qkv, w_gates, w_o, w_out, w_router_t, sinks, ln1_g, ln1_b, router_bias,
           w_gate, w_up, w_down, w_sh_gate, w_sh_up, w_sh_down, ln2_g, ln2_b, b, s):
    n, d = x.shape
    bias_a, bias_b, bias_d = biases
    bw = BRANCH_WIDTH
    tm = min(1024, n)
    proj_a = _matmul(x_bf, w_qkv, layer, 0, PROJ_A_WIDTH // bw, BF16, tm, bw).reshape(b, s, PROJ_A_WIDTH)
    proj = _matmul(x_bf, w_qkv, layer, PROJ_A_WIDTH // bw, PROJ_R_WIDTH // bw, BF16, tm, bw).reshape(b, s, PROJ_R_WIDTH)

    parts = []
    for dil, ba in zip(A_DILATIONS, bias_a):
        o, lse = _band_attention(proj_a, QA, KA, VA, bw, ba, dil)
        parts += [o.reshape(n, bw), lse.reshape(n, bw)]
    o_a = _mix_combine(parts)
    o_b = _moba_attention(proj, bias_b).reshape(n, bw)
    o_c = _sb_attention(proj).reshape(n, bw)
    sink_row = jnp.repeat(sinks.astype(F32), HEAD_DIM)[None, :]
    o_d = _band_attention(proj, QD, KD_COL, VD_COL, KV_W, bias_d, 1, sink_row).reshape(n, bw)

    merged = _merge(x_bf, (o_a, o_b, o_c, o_d), w_gates, w_o, layer)
    x1, x1_bf, logits_t = _out_ln(merged, w_out, x, ln1_g[None, :], ln1_b[None, :], w_router_t, layer)

    tb = EXPERT_BLOCK
    ngroups = MOE_TOKEN_GROUPS if n % (MOE_TOKEN_GROUPS * 512) == 0 else 1
    ng = n // ngroups
    nblk = -(-ng * TOP_K // tb) + N_EXPERTS
    blk_start = jnp.arange(nblk, dtype=jnp.int32) * tb
    token = jnp.broadcast_to(jnp.arange(ng, dtype=jnp.int32)[None, :], (TOP_K, ng))
    routed, xs_groups = [], []
    for gi in range(ngroups):
        dest, gw, counts = _route(logits_t, router_bias, tb, gi * ng, ng)
        pends = jnp.cumsum(jnp.ceil(counts / tb) * tb).astype(jnp.int32)
        blk_expert = jnp.minimum((pends[None, :] <= blk_start[:, None]).sum(1), N_EXPERTS - 1).astype(jnp.int32)
        blk_used = (blk_start < pends[-1]).astype(jnp.int32)
        slot_token = (jnp.arange(nblk * tb, dtype=jnp.int32) % ng).at[dest.reshape(-1)].set(
            token.reshape(-1), unique_indices=True)
        xs_groups.append(x1_bf[slot_token + gi * ng])
        routed.append((dest, gw, blk_expert, blk_used))
    nsh = n // tb
    shared = _experts(x1_bf, jnp.full((nsh,), layer, jnp.int32), jnp.ones((nsh,), jnp.int32),
                      w_sh_gate, w_sh_up, w_sh_down, tb)
    ys_groups = []
    for xs, (dest, gw, blk_expert, blk_used) in zip(xs_groups, routed):
        ys = _experts(xs, blk_expert + layer * N_EXPERTS, blk_used, w_gate, w_up, w_down, tb)
        ys_groups.append(ys[dest])
    gw_all = jnp.concatenate([r[1].T for r in routed], axis=0)
    return _final_ln(x1, shared, ys_groups, gw_all, ln2_g[None, :], ln2_b[None, :])


def kernel(x, rel_bias, w_in, w_o, w_out, attn_sinks, ln1_g, ln1_b, w_router, router_bias,
           w_gate, w_up, w_down, w_sh_gate, w_sh_up, w_sh_down, ln2_g, ln2_b):
    b, s, d = x.shape
    n = b * s
    h = MIX_HEADS
    nl, ne = w_gate.shape[:2]
    hdn = w_gate.shape[3]
    bias_a = [_band_bias(rel_bias[:, :h], win // dil, dil) for win, dil in zip(A_WINDOWS, A_DILATIONS)]
    bias_b = _moba_bias(rel_bias[:, h:2 * h])
    bias_d = _band_bias(rel_bias[:, 2 * h:], SWA_WINDOW - 1, 1)
    w_qkv, w_gates = _cast_w_in(w_in)
    w_o_bf = w_o.astype(BF16).reshape(nl * d, d)
    w_out_bf = w_out.astype(BF16).reshape(nl * d, d)
    w_router_t = w_router.transpose(0, 2, 1).astype(BF16).reshape(nl * ne, d)
    w_gate_s, w_up_s = w_gate.reshape(nl * ne, d, hdn), w_up.reshape(nl * ne, d, hdn)
    w_down_s = w_down.reshape(nl * ne, hdn, d)
    xf = x.reshape(n, d)
    xb = xf.astype(BF16)
    for l in range(nl):
        xf, xb = _layer(l, xf, xb, (bias_a, bias_b, bias_d), w_qkv, w_gates, w_o_bf, w_out_bf, w_router_t,
                        attn_sinks[l], ln1_g[l], ln1_b[l], router_bias[l], w_gate_s, w_up_s, w_down_s,
                        w_sh_gate, w_sh_up, w_sh_down, ln2_g[l], ln2_b[l], b, s)
    return xf.reshape(b, s, d)
```
